```python
import jax, jax.numpy as jnp
from jax import lax
import numpy as np

D_MODEL = 1024
BATCH = 16
SEQ = 2048
DEPTH = 1

MIX_WIDTH = D_MODEL
ATTN_WIDTH = MIX_WIDTH // 2
CONV_WIDTH = MIX_WIDTH - ATTN_WIDTH
HEAD_DIM = 64
N_ATTN_HEADS = ATTN_WIDTH // HEAD_DIM
N_CONV_GROUPS = CONV_WIDTH // HEAD_DIM
CONV_KERNEL = 31
D_FF = 4 * D_MODEL
Q_BLOCK = 128
EPS = 1e-6
IN_COLS = 3 * ATTN_WIDTH + 2 * CONV_WIDTH + N_ATTN_HEADS
SPLITS = (ATTN_WIDTH, 2 * ATTN_WIDTH, 3 * ATTN_WIDTH,
          3 * ATTN_WIDTH + CONV_WIDTH, 3 * ATTN_WIDTH + 2 * CONV_WIDTH)

kernel_name = "hymba_fox_conformer_conv_hybrid"


def rmsnorm(x, g):
    xf = x.astype(jnp.float32)
    y = xf * lax.rsqrt(jnp.mean(xf * xf, axis=-1, keepdims=True) + EPS)
    return (y * g.astype(jnp.float32)).astype(x.dtype)


def layernorm(x, g, b):
    xf = x.astype(jnp.float32)
    mu = jnp.mean(xf, axis=-1, keepdims=True)
    xc = xf - mu
    y = xc * lax.rsqrt(jnp.mean(xc * xc, axis=-1, keepdims=True) + EPS)
    return (y * g.astype(jnp.float32) + b.astype(jnp.float32)).astype(x.dtype)


def headwise_rmsnorm(y, g, n_groups):
    B, S, _ = y.shape
    yh = y.reshape(B, S, n_groups, HEAD_DIM)
    out = rmsnorm(yh, g.reshape(n_groups, HEAD_DIM))
    return out.reshape(B, S, n_groups * HEAD_DIM)


def fox_attention(q, k, v, log_f):
    B, S, H, Dh = q.shape
    scale = Dh ** -0.5
    qh = jnp.transpose(q, (0, 2, 1, 3))
    kh = jnp.transpose(k, (0, 2, 1, 3))
    vh = jnp.transpose(v, (0, 2, 1, 3))
    c = jnp.transpose(jnp.cumsum(log_f.astype(jnp.float32), axis=1), (0, 2, 1))
    outs = []
    for blk in range(S // Q_BLOCK):
        q0 = blk * Q_BLOCK
        q1 = q0 + Q_BLOCK
        qb = qh[:, :, q0:q1]
        kb = kh[:, :, :q1]
        vb = vh[:, :, :q1]
        s = jnp.einsum('bhqd,bhkd->bhqk', qb, kb, preferred_element_type=jnp.float32) * scale
        s = s + c[:, :, q0:q1, None] - c[:, :, None, :q1]
        qpos = q0 + jnp.arange(Q_BLOCK)
        kpos = jnp.arange(q1)
        s = jnp.where(kpos[None, :] <= qpos[:, None], s, -jnp.inf)
        p = jax.nn.softmax(s, axis=-1)
        outs.append(jnp.einsum('bhqk,bhkd->bhqd', p.astype(vb.dtype), vb))
    o = jnp.concatenate(outs, axis=2)
    return jnp.transpose(o, (0, 2, 1, 3))


def causal_depthwise_conv(u, w, b):
    K, C = w.shape
    y = lax.conv_general_dilated(
        u, w[:, None, :].astype(u.dtype), window_strides=(1,), padding=[(K - 1, 0)],
        dimension_numbers=('NWC', 'WIO', 'NWC'), feature_group_count=C)
    return y + b.astype(u.dtype)


def setup_inputs(seed: int = 0) -> dict:
    key = jax.random.key(seed)
    ks = jax.random.split(key, 20)
    f32 = jnp.float32
    L = DEPTH
    nrm = lambda k, shape, s: jax.random.normal(k, shape, f32) * s
    x = jax.random.normal(ks[0], (BATCH, SEQ, D_MODEL), f32)
    norm_mix_g = 1.0 + nrm(ks[1], (L, D_MODEL), 0.02)
    w_in = nrm(ks[2], (L, D_MODEL, IN_COLS), D_MODEL ** -0.5)
    b_forget = jnp.linspace(1.0, 4.0, N_ATTN_HEADS, dtype=f32)[None, :] + nrm(ks[3], (L, N_ATTN_HEADS), 0.1)
    conv_dw_w = nrm(ks[4], (L, CONV_KERNEL, CONV_WIDTH), CONV_KERNEL ** -0.5)
    conv_dw_b = nrm(ks[5], (L, CONV_WIDTH), 0.02)
    conv_ln_g = 1.0 + nrm(ks[6], (L, CONV_WIDTH), 0.02)
    conv_ln_b = nrm(ks[7], (L, CONV_WIDTH), 0.02)
    w_conv_pw = nrm(ks[8], (L, CONV_WIDTH, CONV_WIDTH), CONV_WIDTH ** -0.5)
    b_conv_pw = nrm(ks[9], (L, CONV_WIDTH), 0.02)
    attn_out_g = 1.0 + nrm(ks[10], (L, ATTN_WIDTH), 0.02)
    conv_out_g = 1.0 + nrm(ks[11], (L, CONV_WIDTH), 0.02)
    w_out = nrm(ks[12], (L, MIX_WIDTH, D_MODEL), MIX_WIDTH ** -0.5)
    norm_ffn_g = 1.0 + nrm(ks[13], (L, D_MODEL), 0.02)
    w_ffn_up = nrm(ks[14], (L, D_MODEL, D_FF), D_MODEL ** -0.5)
    w_ffn_down = nrm(ks[15], (L, D_FF, D_MODEL), D_FF ** -0.5)
    norm_final_g = 1.0 + nrm(ks[16], (D_MODEL,), 0.02)
    return {"x": x, "norm_mix_g": norm_mix_g, "w_in": w_in, "b_forget": b_forget,
            "conv_dw_w": conv_dw_w, "conv_dw_b": conv_dw_b, "conv_ln_g": conv_ln_g,
            "conv_ln_b": conv_ln_b, "w_conv_pw": w_conv_pw, "b_conv_pw": b_conv_pw,
            "attn_out_g": attn_out_g, "conv_out_g": conv_out_g, "w_out": w_out,
            "norm_ffn_g": norm_ffn_g, "w_ffn_up": w_ffn_up, "w_ffn_down": w_ffn_down,
            "norm_final_g": norm_final_g}


def reference(x, norm_mix_g, w_in, b_forget, conv_dw_w, conv_dw_b, conv_ln_g, conv_ln_b,
              w_conv_pw, b_conv_pw, attn_out_g, conv_out_g, w_out, norm_ffn_g,
              w_ffn_up, w_ffn_down, norm_final_g):
    B, S, _ = x.shape
    for l in range(DEPTH):
        h = rmsnorm(x, norm_mix_g[l])
        proj = h @ w_in[l]
        q, k, v, ga, gb, f_logit = jnp.split(proj, SPLITS, axis=-1)
        log_f = jax.nn.log_sigmoid(f_logit.astype(jnp.float32) + b_forget[l].astype(jnp.float32))
        hs = (B, S, N_ATTN_HEADS, HEAD_DIM)
        attn = fox_attention(q.reshape(hs), k.reshape(hs), v.reshape(hs), log_f)
        attn = headwise_rmsnorm(attn.reshape(B, S, ATTN_WIDTH), attn_out_g[l], N_ATTN_HEADS)
        u = ga * jax.nn.sigmoid(gb)
        u = causal_depthwise_conv(u, conv_dw_w[l], conv_dw_b[l])
        u = jax.nn.silu(layernorm(u, conv_ln_g[l], conv_ln_b[l]))
        u = u @ w_conv_pw[l] + b_conv_pw[l]
        conv = headwise_rmsnorm(u, conv_out_g[l], N_CONV_GROUPS)
        x = x + jnp.concatenate([attn, conv], axis=-1) @ w_out[l]
        h = rmsnorm(x, norm_ffn_g[l])
        x = x + jnp.square(jax.nn.relu(h @ w_ffn_up[l])) @ w_ffn_down[l]
    return rmsnorm(x, norm_final_g)
```

```python
import functools

import jax
import jax.numpy as jnp
from jax import lax
from jax.experimental import pallas as pl
from jax.experimental.pallas import tpu as pltpu

F32 = jnp.float32
BF16 = jnp.bfloat16

HEAD_DIM = 64
CONV_KERNEL = 31
EPS = 1e-6
LANES = 128
SUBLANES = 8
VMEM_LIMIT = 56 * 1024 * 1024
NEG_BIG = -1e30


def _rms(x, g):
    ms = jnp.mean(x * x, axis=-1, keepdims=True)
    return x * lax.rsqrt(ms + EPS) * g


def _split_dot(a, b_bf16):
    hi = a.astype(BF16)
    lo = (a - hi.astype(F32)).astype(BF16)
    return (jnp.dot(hi, b_bf16, preferred_element_type=F32)
            + jnp.dot(lo, b_bf16, preferred_element_type=F32))


def _in_proj_kernel(x_ref, g_ref, wqkv_ref, wglu_ref, wf_ref, bf_ref,
                    q_ref, k_ref, v_ref, u_ref, lf_ref, *, attn_w, conv_w):
    h = _rms(x_ref[...], g_ref[...]).astype(BF16)
    qkv = jnp.dot(h, wqkv_ref[...], preferred_element_type=F32)
    q_ref[...] = (qkv[:, :attn_w] * (HEAD_DIM ** -0.5)).astype(BF16)
    k_ref[...] = qkv[:, attn_w:2 * attn_w].astype(BF16)
    v_ref[...] = qkv[:, 2 * attn_w:].astype(BF16)
    glu = jnp.dot(h, wglu_ref[...], preferred_element_type=F32)
    ga = glu[:, :conv_w]
    gb = glu[:, conv_w:]
    u_ref[...] = ga * (1.0 / (1.0 + jnp.exp(-gb)))
    fl = jnp.dot(h, wf_ref[...], preferred_element_type=F32) + bf_ref[...]
    lf_ref[...] = jnp.minimum(fl, 0.0) - jnp.log1p(jnp.exp(-jnp.abs(fl)))


def _in_proj(x2, g, wqkv, wglu, wf, bf, *, tm):
    n, d = x2.shape
    attn_w = wqkv.shape[1] // 3
    conv_w = wglu.shape[1] // 2
    const = lambda i: (0, 0)
    row = lambda i: (i, 0)
    return pl.pallas_call(
        functools.partial(_in_proj_kernel, attn_w=attn_w, conv_w=conv_w),
        grid=(n // tm,),
        in_specs=[
            pl.BlockSpec((tm, d), row),
            pl.BlockSpec((1, d), const),
            pl.BlockSpec(wqkv.shape, const),
            pl.BlockSpec(wglu.shape, const),
            pl.BlockSpec(wf.shape, const),
            pl.BlockSpec(bf.shape, const),
        ],
        out_specs=[
            pl.BlockSpec((tm, attn_w), row),
            pl.BlockSpec((tm, attn_w), row),
            pl.BlockSpec((tm, attn_w), row),
            pl.BlockSpec((tm, conv_w), row),
            pl.BlockSpec((tm, LANES), row),
        ],
        out_shape=[
            jax.ShapeDtypeStruct((n, attn_w), BF16),
            jax.ShapeDtypeStruct((n, attn_w), BF16),
            jax.ShapeDtypeStruct((n, attn_w), BF16),
            jax.ShapeDtypeStruct((n, conv_w), F32),
            jax.ShapeDtypeStruct((n, LANES), F32),
        ],
        compiler_params=pltpu.CompilerParams(
            dimension_semantics=("arbitrary",), vmem_limit_bytes=VMEM_LIMIT),
        name="in_proj",
    )(x2, g, wqkv, wglu, wf, bf)


def _cumsum_kernel(lf_ref, c_ref):
    heads, seq = lf_ref.shape[1], lf_ref.shape[2]
    lane = lax.broadcasted_iota(jnp.int32, (heads, LANES), 1)
    carry = jnp.zeros((heads, 1), F32)
    for c in range(seq // LANES):
        x = lf_ref[0, :, c * LANES:(c + 1) * LANES]
        sh = 1
        while sh < LANES:
            x = x + jnp.where(lane >= sh, pltpu.roll(x, sh, axis=1), 0.0)
            sh *= 2
        x = x + carry
        c_ref[0, :, c * LANES:(c + 1) * LANES] = x
        carry = x[:, LANES - 1:LANES]


def _cumsum(lf_t):
    b, h, s = lf_t.shape
    spec = pl.BlockSpec((1, h, s), lambda i: (i, 0, 0))
    return pl.pallas_call(
        _cumsum_kernel,
        grid=(b,),
        in_specs=[spec],
        out_specs=spec,
        out_shape=jax.ShapeDtypeStruct((b, h, s), F32),
        compiler_params=pltpu.CompilerParams(dimension_semantics=("arbitrary",)),
        name="cumsum",
    )(lf_t)


def _attn_kernel(q_ref, k_ref, v_ref, crow_ref, ccol_ref, g_ref, gsum_ref, o_ref,
                 m_ref, l_ref, acc_ref, *, tq, tk):
    hp = pl.program_id(1)
    i = pl.program_id(2)
    q2 = q_ref[0]
    lane = lax.broadcasted_iota(jnp.int32, (tq, LANES), 1)
    first = lane < HEAD_DIM
    row = lax.broadcasted_iota(jnp.int32, (tq, tk), 0)
    col = lax.broadcasted_iota(jnp.int32, (tq, tk), 1)
    n_sub = tq // tk

    for e in range(2):
        qe = jnp.where(first if e == 0 else jnp.logical_not(first), q2, jnp.zeros_like(q2))
        ct = ccol_ref[0, e]
        m_ref[...] = jnp.full(m_ref.shape, NEG_BIG, F32)
        l_ref[...] = jnp.zeros(l_ref.shape, F32)
        acc_ref[...] = jnp.zeros(acc_ref.shape, F32)

        def update(j, diag_offset):
            k0 = pl.multiple_of(j * tk, tk)
            kt = k_ref[0, pl.ds(k0, tk), :]
            vt = v_ref[0, pl.ds(k0, tk), :]
            s = lax.dot_general(qe, kt, (((1,), (1,)), ((), ())),
                                preferred_element_type=F32)
            cs = crow_ref[0, pl.ds(2 * hp + e, 1), pl.ds(k0, tk)]
            s = s + (ct - cs)
            if diag_offset is not None:
                s = jnp.where(col + diag_offset <= row, s, -jnp.inf)
            m_old = m_ref[...]
            m_new = jnp.maximum(m_old, jnp.max(s, axis=-1, keepdims=True))
            alpha = jnp.exp(m_old - m_new)
            p = jnp.exp(s - m_new)
            l_ref[...] = alpha * l_ref[...] + jnp.sum(p, axis=-1, keepdims=True)
            acc_ref[...] = alpha * acc_ref[...] + jnp.dot(
                p.astype(BF16), vt, preferred_element_type=F32)
            m_ref[...] = m_new

        def body(j, carry):
            update(j, None)
            return carry

        lax.fori_loop(0, i * n_sub, body, 0)
        for d in range(n_sub):
            update(i * n_sub + d, d * tk)

        o_e = acc_ref[...] / l_ref[...]
        if e == 0:
            o = o_e
        else:
            o = jnp.where(first, o, o_e)

    ms = _split_dot(o * o, gsum_ref[...]) * (1.0 / HEAD_DIM)
    o_ref[0] = (o * lax.rsqrt(ms + EPS) * g_ref[...]).astype(o_ref.dtype)


def _attention(q, k, v, c_row, c_col, g, gsum, *, tq, tk):
    b, s, w = q.shape
    heads = w // HEAD_DIM
    return pl.pallas_call(
        functools.partial(_attn_kernel, tq=tq, tk=tk),
        grid=(b, heads // 2, s // tq),
        in_specs=[
            pl.BlockSpec((1, tq, LANES), lambda bi, hp, i: (bi, i, hp)),
            pl.BlockSpec((1, s, LANES), lambda bi, hp, i: (bi, 0, hp)),
            pl.BlockSpec((1, s, LANES), lambda bi, hp, i: (bi, 0, hp)),
            pl.BlockSpec((1, heads, s), lambda bi, hp, i: (bi, 0, 0)),
            pl.BlockSpec((1, 2, tq, 1), lambda bi, hp, i: (bi, hp, i, 0)),
            pl.BlockSpec((1, LANES), lambda bi, hp, i: (0, hp)),
            pl.BlockSpec((LANES, LANES), lambda bi, hp, i: (0, 0)),
        ],
        out_specs=pl.BlockSpec((1, tq, LANES), lambda bi, hp, i: (bi, i, hp)),
        out_shape=jax.ShapeDtypeStruct((b, s, w), BF16),
        scratch_shapes=[
            pltpu.VMEM((tq, 1), F32),
            pltpu.VMEM((tq, 1), F32),
            pltpu.VMEM((tq, LANES), F32),
        ],
        compiler_params=pltpu.CompilerParams(
            dimension_semantics=("arbitrary", "arbitrary", "arbitrary"),
            vmem_limit_bytes=VMEM_LIMIT),
        name="fox_attention",
    )(q, k, v, c_row, c_col, g, gsum)


def _conv_kernel(u_ref, dw_ref, db_ref, lg_ref, lb_ref, pw_ref, pb_ref, og_ref, gsum_ref,
                 o_ref, pad_ref, *, rows, halo):
    seq, width = u_ref.shape[1], u_ref.shape[2]
    pad_ref[0:halo, :] = jnp.zeros((halo, width), F32)
    pad_ref[halo:, :] = u_ref[0]
    for r in range(seq // rows):
        r0 = r * rows
        acc = jnp.zeros((rows, width), F32) + db_ref[...]
        for t in range(CONV_KERNEL):
            start = r0 + halo - (CONV_KERNEL - 1) + t
            acc = acc + pad_ref[start:start + rows, :] * dw_ref[t:t + 1, :]
        mu = jnp.mean(acc, axis=-1, keepdims=True)
        xc = acc - mu
        var = jnp.mean(xc * xc, axis=-1, keepdims=True)
        y = xc * lax.rsqrt(var + EPS) * lg_ref[...] + lb_ref[...]
        y = y * (1.0 / (1.0 + jnp.exp(-y)))
        z = jnp.dot(y.astype(BF16), pw_ref[...], preferred_element_type=F32) + pb_ref[...]
        ms = _split_dot(z * z, gsum_ref[...]) * (1.0 / HEAD_DIM)
        o_ref[0, r0:r0 + rows, :] = (z * lax.rsqrt(ms + EPS) * og_ref[...]).astype(o_ref.dtype)


def _conv(u, dw, db, lg, lb, pw, pb, og, gsum, *, rows):
    b, s, w = u.shape
    halo = 32
    const = lambda i: (0, 0)
    vec = pl.BlockSpec((1, w), const)
    return pl.pallas_call(
        functools.partial(_conv_kernel, rows=rows, halo=halo),
        grid=(b,),
        in_specs=[
            pl.BlockSpec((1, s, w), lambda i: (i, 0, 0)),
            pl.BlockSpec(dw.shape, const),
            vec, vec, vec,
            pl.BlockSpec(pw.shape, const),
            vec, vec,
            pl.BlockSpec(gsum.shape, const),
        ],
        out_specs=pl.BlockSpec((1, s, w), lambda i: (i, 0, 0)),
        out_shape=jax.ShapeDtypeStruct((b, s, w), BF16),
        scratch_shapes=[pltpu.VMEM((s + halo, w), F32)],
        compiler_params=pltpu.CompilerParams(
            dimension_semantics=("arbitrary",), vmem_limit_bytes=VMEM_LIMIT),
        name="conv_module",
    )(u, dw, db, lg, lb, pw, pb, og, gsum)


def _post_kernel(x_ref, a_ref, c_ref, woa_ref, woc_ref, gf_ref, wup_ref, wdn_ref, gl_ref,
                 o_ref, *, ff_chunk, apply_final_norm):
    x1 = (x_ref[...]
          + jnp.dot(a_ref[...], woa_ref[...], preferred_element_type=F32)
          + jnp.dot(c_ref[...], woc_ref[...], preferred_element_type=F32))
    h = _rms(x1, gf_ref[...]).astype(BF16)
    d_ff = wup_ref.shape[1]
    y = x1
    for c in range(d_ff // ff_chunk):
        a = jnp.dot(h, wup_ref[:, c * ff_chunk:(c + 1) * ff_chunk], preferred_element_type=F32)
        a = jnp.maximum(a, 0.0)
        a = (a * a).astype(BF16)
        y = y + jnp.dot(a, wdn_ref[c * ff_chunk:(c + 1) * ff_chunk, :], preferred_element_type=F32)
    o_ref[...] = _rms(y, gl_ref[...]) if apply_final_norm else y


def _post(x2, attn, conv, woa, woc, gf, wup, wdn, gl, *, tm, ff_chunk, apply_final_norm):
    n, d = x2.shape
    const = lambda i: (0, 0)
    row = lambda i: (i, 0)
    return pl.pallas_call(
        functools.partial(_post_kernel, ff_chunk=ff_chunk, apply_final_norm=apply_final_norm),
        grid=(n // tm,),
        in_specs=[
            pl.BlockSpec((tm, d), row),
            pl.BlockSpec((tm, attn.shape[1]), row),
            pl.BlockSpec((tm, conv.shape[1]), row),
            pl.BlockSpec(woa.shape, const),
            pl.BlockSpec(woc.shape, const),
            pl.BlockSpec((1, d), const),
            pl.BlockSpec(wup.shape, const),
            pl.BlockSpec(wdn.shape, const),
            pl.BlockSpec((1, d), const),
        ],
        out_specs=pl.BlockSpec((tm, d), row),
        out_shape=jax.ShapeDtypeStruct((n, d), F32),
        compiler_params=pltpu.CompilerParams(
            dimension_semantics=("arbitrary",), vmem_limit_bytes=VMEM_LIMIT),
        name="out_proj_ffn",
    )(x2, attn, conv, woa, woc, gf, wup, wdn, gl)


def _group_sum_matrix(width):
    idx = jnp.arange(width) // HEAD_DIM
    return (idx[:, None] == idx[None, :]).astype(BF16)


def kernel(x, norm_mix_g, w_in, b_forget, conv_dw_w, conv_dw_b, conv_ln_g, conv_ln_b,
           w_conv_pw, b_conv_pw, attn_out_g, conv_out_g, w_out, norm_ffn_g,
           w_ffn_up, w_ffn_down, norm_final_g):
    b, s, d = x.shape
    depth = w_in.shape[0]
    heads = b_forget.shape[1]
    attn_w = heads * HEAD_DIM
    conv_w = conv_dw_w.shape[2]
    n = b * s
    x2 = x.reshape(n, d)
    gsum_attn = _group_sum_matrix(LANES)
    gsum_conv = _group_sum_matrix(conv_w)
    for l in range(depth):
        w = w_in[l]
        wqkv = w[:, :3 * attn_w].astype(BF16)
        wglu = w[:, 3 * attn_w:3 * attn_w + 2 * conv_w].astype(BF16)
        wf = jnp.pad(w[:, 3 * attn_w + 2 * conv_w:], ((0, 0), (0, LANES - heads))).astype(BF16)
        bfg = jnp.pad(b_forget[l], (0, LANES - heads)).reshape(1, LANES)
        q, k, v, u, lf = _in_proj(x2, norm_mix_g[l].reshape(1, d), wqkv, wglu, wf, bfg, tm=512)

        lf_t = jnp.transpose(lf[:, :heads].reshape(b, s, heads), (0, 2, 1))
        c_row = _cumsum(lf_t)
        c_col = c_row.reshape(b, heads, s, 1)
        attn = _attention(q.reshape(b, s, attn_w), k.reshape(b, s, attn_w),
                          v.reshape(b, s, attn_w), c_row, c_col,
                          attn_out_g[l].reshape(1, attn_w), gsum_attn, tq=256, tk=256)

        vec = lambda a: a.reshape(1, conv_w)
        conv = _conv(u.reshape(b, s, conv_w), conv_dw_w[l], vec(conv_dw_b[l]),
                     vec(conv_ln_g[l]), vec(conv_ln_b[l]), w_conv_pw[l].astype(BF16),
                     vec(b_conv_pw[l]), vec(conv_out_g[l]), gsum_conv, rows=256)

        wo = w_out[l].astype(BF16)
        x2 = _post(x2, attn.reshape(n, attn_w), conv.reshape(n, conv_w),
                   wo[:attn_w], wo[attn_w:], norm_ffn_g[l].reshape(1, d),
                   w_ffn_up[l].astype(BF16), w_ffn_down[l].astype(BF16),
                   norm_final_g.reshape(1, d), tm=256, ff_chunk=w_ffn_up.shape[2],
                   apply_final_norm=(l == depth - 1))
    return x2.reshape(b, s, d)
```

```python
import functools

import jax
import jax.numpy as jnp
from jax import lax
from jax.experimental import pallas as pl
from jax.experimental.pallas import tpu as pltpu

F32 = jnp.float32
BF16 = jnp.bfloat16

HEAD_DIM = 64
CONV_KERNEL = 31
EPS = 1e-6
LANES = 128
SUBLANES = 8
VMEM_LIMIT = 56 * 1024 * 1024
NEG_BIG = -1e30
N_SPLIT = 3
NT_DIMS = (((1,), (1,)), ((), ()))
TN_DIMS = (((0,), (0,)), ((), ()))


def _rms(x, g):
    ms = jnp.mean(x * x, axis=-1, keepdims=True)
    return x * lax.rsqrt(ms + EPS) * g


def _split3(x):
    hi = x.astype(BF16)
    r = x - hi.astype(F32)
    mid = r.astype(BF16)
    lo = (r - mid.astype(F32)).astype(BF16)
    return hi, mid, lo


def _split_dot(a, b_bf16):
    hi = a.astype(BF16)
    lo = (a - hi.astype(F32)).astype(BF16)
    return (jnp.dot(hi, b_bf16, preferred_element_type=F32)
            + jnp.dot(lo, b_bf16, preferred_element_type=F32))


def _in_proj_kernel(x_ref, g_ref, wqv_t_ref, wrest_ref, bf_ref,
                    qt_ref, k_ref, vt_ref, u_ref, lf_ref, *, attn_w, conv_w):
    h = _rms(x_ref[...], g_ref[...]).astype(BF16)
    qv_t = lax.dot_general(wqv_t_ref[...], h, NT_DIMS, preferred_element_type=F32)
    qt_ref[...] = (qv_t[:attn_w] * (HEAD_DIM ** -0.5)).astype(BF16)
    vt_ref[...] = qv_t[attn_w:].astype(BF16)
    rest = jnp.dot(h, wrest_ref[...], preferred_element_type=F32)
    k_ref[...] = rest[:, :attn_w].astype(BF16)
    ga = rest[:, attn_w:attn_w + conv_w]
    gb = rest[:, attn_w + conv_w:attn_w + 2 * conv_w]
    u_ref[...] = ga * (1.0 / (1.0 + jnp.exp(-gb)))
    fl = rest[:, attn_w + 2 * conv_w:] + bf_ref[...]
    lf_ref[...] = jnp.minimum(fl, 0.0) - jnp.log1p(jnp.exp(-jnp.abs(fl)))


def _in_proj(x2, g, wqv_t, wrest, bf, *, attn_w, conv_w, tm):
    n, d = x2.shape
    const = lambda i: (0, 0)
    row = lambda i: (i, 0)
    col = lambda i: (0, i)
    return pl.pallas_call(
        functools.partial(_in_proj_kernel, attn_w=attn_w, conv_w=conv_w),
        grid=(n // tm,),
        in_specs=[
            pl.BlockSpec((tm, d), row),
            pl.BlockSpec((1, d), const),
            pl.BlockSpec(wqv_t.shape, const),
            pl.BlockSpec(wrest.shape, const),
            pl.BlockSpec(bf.shape, const),
        ],
        out_specs=[
            pl.BlockSpec((attn_w, tm), col),
            pl.BlockSpec((tm, attn_w), row),
            pl.BlockSpec((attn_w, tm), col),
            pl.BlockSpec((tm, conv_w), row),
            pl.BlockSpec((tm, LANES), row),
        ],
        out_shape=[
            jax.ShapeDtypeStruct((attn_w, n), BF16),
            jax.ShapeDtypeStruct((n, attn_w), BF16),
            jax.ShapeDtypeStruct((attn_w, n), BF16),
            jax.ShapeDtypeStruct((n, conv_w), F32),
            jax.ShapeDtypeStruct((n, LANES), F32),
        ],
        compiler_params=pltpu.CompilerParams(
            dimension_semantics=("arbitrary",), vmem_limit_bytes=VMEM_LIMIT),
        name="in_proj",
    )(x2, g, wqv_t, wrest, bf)


def _cumsum_kernel(lf_ref, tri_ref, scat_ref, ones_ref, kaug_ref, crow_ref, *, blk, heads):
    seq = lf_ref.shape[1]
    nblk = seq // blk
    tri = tri_ref[...]
    local = []
    for r in range(nblk):
        parts = _split3(lf_ref[0, r * blk:(r + 1) * blk, :])
        local.append(sum(jnp.dot(tri, p, preferred_element_type=F32) for p in parts))
    offset = jnp.zeros((1, LANES), F32)
    for r in range(nblk):
        c = local[r] + offset
        offset = c[blk - 1:blk, :]
        aug = ones_ref[...] + sum(
            jnp.dot(p, scat_ref[t], preferred_element_type=F32)
            for t, p in enumerate(_split3(-c)))
        kaug_ref[0, r * blk:(r + 1) * blk, :] = aug.astype(BF16)
        crow_ref[0, :, r * blk:(r + 1) * blk] = c.T[:heads, :]


def _forget_cumsum(lf3, heads, *, blk):
    b, s, _ = lf3.shape
    tri = (jnp.arange(blk)[:, None] >= jnp.arange(blk)[None, :]).astype(BF16)
    lane = jnp.arange(LANES)
    scat = jnp.stack([
        ((lane[None, :] == N_SPLIT + N_SPLIT * lane[:, None] + t) & (lane[:, None] < heads)).astype(BF16)
        for t in range(N_SPLIT)])
    ones_row = (lane < N_SPLIT).astype(F32).reshape(1, LANES)
    const2 = lambda i: (0, 0)
    return pl.pallas_call(
        functools.partial(_cumsum_kernel, blk=blk, heads=heads),
        grid=(b,),
        in_specs=[
            pl.BlockSpec((1, s, LANES), lambda i: (i, 0, 0)),
            pl.BlockSpec(tri.shape, const2),
            pl.BlockSpec(scat.shape, lambda i: (0, 0, 0)),
            pl.BlockSpec(ones_row.shape, const2),
        ],
        out_specs=[
            pl.BlockSpec((1, s, LANES), lambda i: (i, 0, 0)),
            pl.BlockSpec((1, heads, s), lambda i: (i, 0, 0)),
        ],
        out_shape=[
            jax.ShapeDtypeStruct((b, s, LANES), BF16),
            jax.ShapeDtypeStruct((b, heads, s), F32),
        ],
        compiler_params=pltpu.CompilerParams(dimension_semantics=("arbitrary",)),
        name="forget_cumsum",
    )(lf3, tri, scat, ones_row)


def _attn_kernel(qt_ref, k_ref, kaug_ref, vt_ref, crow_ref, g_ref, o_ref,
                 qrhs_ref, m_ref, l_ref, acc_ref, *, tq, heads):
    tk = tq
    i = pl.program_id(1)
    rowid = lax.broadcasted_iota(jnp.int32, (LANES, tq), 0)
    for h in range(heads):
        pair, e = divmod(h, 2)
        q_pair = qt_ref[pair * LANES:(pair + 1) * LANES, :]
        own = (rowid >= HEAD_DIM * e) & (rowid < HEAD_DIM * (e + 1))
        qrhs_ref[h, 0:LANES, :] = jnp.where(own, q_pair, jnp.zeros_like(q_pair))
        c_hi, c_mid, c_lo = (t.astype(F32) for t in _split3(crow_ref[0, h:h + 1, :]))
        own_lanes = (rowid >= N_SPLIT * (h + 1)) & (rowid < N_SPLIT * (h + 2))
        aug = jnp.where(rowid == 0, c_hi,
                        jnp.where(rowid == 1, c_mid,
                                  jnp.where(rowid == 2, c_lo,
                                            jnp.where(own_lanes, 1.0, 0.0))))
        qrhs_ref[h, LANES:2 * LANES, :] = aug.astype(BF16)
    m_ref[...] = jnp.full(m_ref.shape, NEG_BIG, F32)
    l_ref[...] = jnp.zeros(l_ref.shape, F32)
    acc_ref[...] = jnp.zeros(acc_ref.shape, F32)

    key_pos = lax.broadcasted_iota(jnp.int32, (tk, tq), 0)
    qry_pos = lax.broadcasted_iota(jnp.int32, (tk, tq), 1)
    causal = key_pos <= qry_pos

    def update(j, masked):
        k0 = pl.multiple_of(j * tk, tk)
        kaug = kaug_ref[0, pl.ds(k0, tk), :]

        def scores(h):
            pair = h // 2
            k_lhs = jnp.concatenate(
                [k_ref[0, pl.ds(k0, tk), pair * LANES:(pair + 1) * LANES], kaug], axis=1)
            return jnp.dot(k_lhs, qrhs_ref[h], preferred_element_type=F32)

        ahead = 8
        s_all = {h: scores(h) for h in range(ahead)}
        for h in range(heads):
            if h + ahead < heads:
                s_all[h + ahead] = scores(h + ahead)
            pair, e = divmod(h, 2)
            s_t = s_all.pop(h)
            if masked:
                s_t = jnp.where(causal, s_t, NEG_BIG)
            m_old = m_ref[h:h + 1, :]
            m_new = jnp.maximum(m_old, jnp.max(s_t, axis=0, keepdims=True))
            alpha = jnp.exp(m_old - m_new)
            p = jnp.exp(s_t - m_new)
            l_ref[h:h + 1, :] = alpha * l_ref[h:h + 1, :] + jnp.sum(p, axis=0, keepdims=True)
            v_t = vt_ref[pair * LANES:(pair + 1) * LANES, pl.ds(k0, tk)]
            pv = jnp.dot(v_t, p.astype(BF16), preferred_element_type=F32)
            rows = slice(h * HEAD_DIM, (h + 1) * HEAD_DIM)
            acc_ref[rows, :] = alpha * acc_ref[rows, :] + pv[e * HEAD_DIM:(e + 1) * HEAD_DIM, :]
            m_ref[h:h + 1, :] = m_new

    def body(j, carry):
        update(j, False)
        return carry

    lax.fori_loop(0, i, body, 0)
    update(i, True)

    for h in range(heads):
        rows = slice(h * HEAD_DIM, (h + 1) * HEAD_DIM)
        o = acc_ref[rows, :] / l_ref[h:h + 1, :]
        ms = jnp.mean(o * o, axis=0, keepdims=True)
        o_ref[rows, :] = (o * lax.rsqrt(ms + EPS) * g_ref[rows, :]).astype(o_ref.dtype)


def _attention(q_t, k3, kaug, v_t, c_row, g_col, *, tq):
    b, s, w = k3.shape
    heads = w // HEAD_DIM
    nq = s // tq
    return pl.pallas_call(
        functools.partial(_attn_kernel, tq=tq, heads=heads),
        grid=(b, nq),
        in_specs=[
            pl.BlockSpec((w, tq), lambda bi, i: (0, bi * nq + i)),
            pl.BlockSpec((1, s, w), lambda bi, i: (bi, 0, 0)),
            pl.BlockSpec((1, s, LANES), lambda bi, i: (bi, 0, 0)),
            pl.BlockSpec((w, s), lambda bi, i: (0, bi)),
            pl.BlockSpec((1, heads, tq), lambda bi, i: (bi, 0, i)),
            pl.BlockSpec((w, 1), lambda bi, i: (0, 0)),
        ],
        out_specs=pl.BlockSpec((w, tq), lambda bi, i: (0, bi * nq + i)),
        out_shape=jax.ShapeDtypeStruct((w, b * s), BF16),
        scratch_shapes=[
            pltpu.VMEM((heads, 2 * LANES, tq), BF16),
            pltpu.VMEM((heads, tq), F32),
            pltpu.VMEM((heads, tq), F32),
            pltpu.VMEM((w, tq), F32),
        ],
        compiler_params=pltpu.CompilerParams(
            dimension_semantics=("arbitrary", "arbitrary"), vmem_limit_bytes=VMEM_LIMIT),
        name="fox_attention",
    )(q_t, k3, kaug, v_t, c_row, g_col)


def _conv_kernel(u_ref, dw_ref, db_ref, lg_ref, lb_ref, pw_ref, pb_ref, og_ref, gsum_ref,
                 o_ref, pad_ref, y_ref, *, rows, sub_rows, halo):
    seq, width = u_ref.shape[1], u_ref.shape[2]
    pad_ref[0:halo, :] = jnp.zeros((halo, width), F32)
    pad_ref[halo:, :] = u_ref[0]

    lead = halo - (CONV_KERNEL - 1)
    window = sub_rows + halo

    def block(r, carry):
        r0 = pl.multiple_of(r * rows, rows)
        for sb in range(rows // sub_rows):
            for cb in range(width // LANES):
                lanes = slice(cb * LANES, (cb + 1) * LANES)
                x_win = pad_ref[pl.ds(r0 + sb * sub_rows, window), lanes]
                acc = jnp.zeros((sub_rows, LANES), F32) + db_ref[:, lanes]
                for res in range(SUBLANES):
                    taps = [t for t in range(CONV_KERNEL) if (lead + t) % SUBLANES == res]
                    shifted = x_win if res == 0 else pltpu.roll(x_win, window - res, axis=0)
                    for t in taps:
                        a0 = (lead + t) - res
                        acc = acc + shifted[a0:a0 + sub_rows, :] * dw_ref[t:t + 1, lanes]
                y_ref[sb * sub_rows:(sb + 1) * sub_rows, lanes] = acc
        acc = y_ref[...]
        mu = jnp.mean(acc, axis=-1, keepdims=True)
        xc = acc - mu
        var = jnp.mean(xc * xc, axis=-1, keepdims=True)
        y = xc * lax.rsqrt(var + EPS) * lg_ref[...] + lb_ref[...]
        y = y * (1.0 / (1.0 + jnp.exp(-y)))
        z = jnp.dot(y.astype(BF16), pw_ref[...], preferred_element_type=F32) + pb_ref[...]
        ms = _split_dot(z * z, gsum_ref[...]) * (1.0 / HEAD_DIM)
        o_ref[0, pl.ds(r0, rows), :] = (z * lax.rsqrt(ms + EPS) * og_ref[...]).astype(o_ref.dtype)
        return carry

    lax.fori_loop(0, seq // rows, block, 0)


def _conv(u, dw, db, lg, lb, pw, pb, og, gsum, *, rows):
    b, s, w = u.shape
    halo = 32
    const = lambda i: (0, 0)
    vec = pl.BlockSpec((1, w), const)
    return pl.pallas_call(
        functools.partial(_conv_kernel, rows=rows, sub_rows=rows // 2, halo=halo),
        grid=(b,),
        in_specs=[
            pl.BlockSpec((1, s, w), lambda i: (i, 0, 0)),
            pl.BlockSpec(dw.shape, const),
            vec, vec, vec,
            pl.BlockSpec(pw.shape, const),
            vec, vec,
            pl.BlockSpec(gsum.shape, const),
        ],
        out_specs=pl.BlockSpec((1, s, w), lambda i: (i, 0, 0)),
        out_shape=jax.ShapeDtypeStruct((b, s, w), BF16),
        scratch_shapes=[pltpu.VMEM((s + halo, w), F32), pltpu.VMEM((rows, w), F32)],
        compiler_params=pltpu.CompilerParams(
            dimension_semantics=("arbitrary",), vmem_limit_bytes=VMEM_LIMIT),
        name="conv_module",
    )(u, dw, db, lg, lb, pw, pb, og, gsum)


def _post_kernel(x_ref, at_ref, c_ref, woa_ref, woc_ref, gf_ref, wup_ref, wdn_ref, gl_ref,
                 o_ref, *, apply_final_norm):
    x1 = (x_ref[...]
          + lax.dot_general(at_ref[...], woa_ref[...], TN_DIMS, preferred_element_type=F32)
          + jnp.dot(c_ref[...], woc_ref[...], preferred_element_type=F32))
    h = _rms(x1, gf_ref[...]).astype(BF16)
    a = jnp.maximum(jnp.dot(h, wup_ref[...], preferred_element_type=F32), 0.0)
    y = x1 + jnp.dot((a * a).astype(BF16), wdn_ref[...], preferred_element_type=F32)
    o_ref[...] = _rms(y, gl_ref[...]) if apply_final_norm else y


def _post(x2, attn_t, conv, woa, woc, gf, wup, wdn, gl, *, tm, apply_final_norm):
    n, d = x2.shape
    const = lambda i: (0, 0)
    row = lambda i: (i, 0)
    return pl.pallas_call(
        functools.partial(_post_kernel, apply_final_norm=apply_final_norm),
        grid=(n // tm,),
        in_specs=[
            pl.BlockSpec((tm, d), row),
            pl.BlockSpec((attn_t.shape[0], tm), lambda i: (0, i)),
            pl.BlockSpec((tm, conv.shape[1]), row),
            pl.BlockSpec(woa.shape, const),
            pl.BlockSpec(woc.shape, const),
            pl.BlockSpec((1, d), const),
            pl.BlockSpec(wup.shape, const),
            pl.BlockSpec(wdn.shape, const),
            pl.BlockSpec((1, d), const),
        ],
        out_specs=pl.BlockSpec((tm, d), row),
        out_shape=jax.ShapeDtypeStruct((n, d), F32),
        compiler_params=pltpu.CompilerParams(
            dimension_semantics=("arbitrary",), vmem_limit_bytes=VMEM_LIMIT),
        name="out_proj_ffn",
    )(x2, attn_t, conv, woa, woc, gf, wup, wdn, gl)


def _group_sum_matrix(width):
    idx = jnp.arange(width) // HEAD_DIM
    return (idx[:, None] == idx[None, :]).astype(BF16)


def kernel(x, norm_mix_g, w_in, b_forget, conv_dw_w, conv_dw_b, conv_ln_g, conv_ln_b,
           w_conv_pw, b_conv_pw, attn_out_g, conv_out_g, w_out, norm_ffn_g,
           w_ffn_up, w_ffn_down, norm_final_g):
    b, s, d = x.shape
    depth = w_in.shape[0]
    heads = b_forget.shape[1]
    attn_w = heads * HEAD_DIM
    conv_w = conv_dw_w.shape[2]
    n = b * s
    x2 = x.reshape(n, d)
    gsum_conv = _group_sum_matrix(conv_w)
    for l in range(depth):
        w = w_in[l]
        wq, wk, wv = w[:, :attn_w], w[:, attn_w:2 * attn_w], w[:, 2 * attn_w:3 * attn_w]
        wqv_t = jnp.concatenate([wq, wv], axis=1).T.astype(BF16)
        wf = jnp.pad(w[:, 3 * attn_w + 2 * conv_w:], ((0, 0), (0, LANES - heads)))
        wrest = jnp.concatenate(
            [wk, w[:, 3 * attn_w:3 * attn_w + 2 * conv_w], wf], axis=1).astype(BF16)
        bfg = jnp.pad(b_forget[l], (0, LANES - heads)).reshape(1, LANES)
        q_t, k, v_t, u, lf = _in_proj(x2, norm_mix_g[l].reshape(1, d), wqv_t, wrest, bfg,
                                      attn_w=attn_w, conv_w=conv_w, tm=512)

        kaug, c_row = _forget_cumsum(lf.reshape(b, s, LANES), heads, blk=256)
        attn_t = _attention(q_t, k.reshape(b, s, attn_w), kaug, v_t, c_row,
                            attn_out_g[l].reshape(attn_w, 1), tq=256)

        vec = lambda a: a.reshape(1, conv_w)
        conv = _conv(u.reshape(b, s, conv_w), conv_dw_w[l], vec(conv_dw_b[l]),
                     vec(conv_ln_g[l]), vec(conv_ln_b[l]), w_conv_pw[l].astype(BF16),
                     vec(b_conv_pw[l]), vec(conv_out_g[l]), gsum_conv, rows=256)

        wo = w_out[l].astype(BF16)
        x2 = _post(x2, attn_t, conv.reshape(n, conv_w),
                   wo[:attn_w], wo[attn_w:], norm_ffn_g[l].reshape(1, d),
                   w_ffn_up[l].astype(BF16), w_ffn_down[l].astype(BF16),
                   norm_final_g.reshape(1, d), tm=256,
                   apply_final_norm=(l == depth - 1))
    return x2.reshape(b, s, d)
```

```python
import functools

import jax
import jax.numpy as jnp
from jax import lax
from jax.experimental import pallas as pl
from jax.experimental.pallas import tpu as pltpu

F32 = jnp.float32
BF16 = jnp.bfloat16

HEAD_DIM = 64
CONV_KERNEL = 31
EPS = 1e-6
LANES = 128
SUBLANES = 8
VMEM_LIMIT = 56 * 1024 * 1024
NEG_BIG = -1e30
N_SPLIT = 3
NT_DIMS = (((1,), (1,)), ((), ()))
TN_DIMS = (((0,), (0,)), ((), ()))


def _rms(x, g):
    ms = jnp.mean(x * x, axis=-1, keepdims=True)
    return x * lax.rsqrt(ms + EPS) * g


def _split3(x):
    hi = x.astype(BF16)
    r = x - hi.astype(F32)
    mid = r.astype(BF16)
    lo = (r - mid.astype(F32)).astype(BF16)
    return hi, mid, lo


def _split_dot(a, b_bf16):
    hi = a.astype(BF16)
    lo = (a - hi.astype(F32)).astype(BF16)
    return (jnp.dot(hi, b_bf16, preferred_element_type=F32)
            + jnp.dot(lo, b_bf16, preferred_element_type=F32))


def _in_proj_kernel(x_ref, g_ref, wqv_t_ref, wrest_ref, bf_ref,
                    qt_ref, k_ref, vt_ref, u_ref, lf_ref, *, attn_w, conv_w):
    h = _rms(x_ref[...], g_ref[...]).astype(BF16)
    qv_t = lax.dot_general(wqv_t_ref[...], h, NT_DIMS, preferred_element_type=F32)
    qt_ref[...] = (qv_t[:attn_w] * (HEAD_DIM ** -0.5)).astype(BF16)
    vt_ref[...] = qv_t[attn_w:].astype(BF16)
    rest = jnp.dot(h, wrest_ref[...], preferred_element_type=F32)
    k_ref[...] = rest[:, :attn_w].astype(BF16)
    ga = rest[:, attn_w:attn_w + conv_w]
    gb = rest[:, attn_w + conv_w:attn_w + 2 * conv_w]
    u_ref[...] = ga * (1.0 / (1.0 + jnp.exp(-gb)))
    fl = rest[:, attn_w + 2 * conv_w:] + bf_ref[...]
    lf_ref[...] = jnp.minimum(fl, 0.0) - jnp.log1p(jnp.exp(-jnp.abs(fl)))


def _in_proj(x2, g, wqv_t, wrest, bf, *, attn_w, conv_w, tm):
    n, d = x2.shape
    const = lambda i: (0, 0)
    row = lambda i: (i, 0)
    col = lambda i: (0, i)
    return pl.pallas_call(
        functools.partial(_in_proj_kernel, attn_w=attn_w, conv_w=conv_w),
        grid=(n // tm,),
        in_specs=[
            pl.BlockSpec((tm, d), row),
            pl.BlockSpec((1, d), const),
            pl.BlockSpec(wqv_t.shape, const),
            pl.BlockSpec(wrest.shape, const),
            pl.BlockSpec(bf.shape, const),
        ],
        out_specs=[
            pl.BlockSpec((attn_w, tm), col),
            pl.BlockSpec((tm, attn_w), row),
            pl.BlockSpec((attn_w, tm), col),
            pl.BlockSpec((tm, conv_w), row),
            pl.BlockSpec((tm, LANES), row),
        ],
        out_shape=[
            jax.ShapeDtypeStruct((attn_w, n), BF16),
            jax.ShapeDtypeStruct((n, attn_w), BF16),
            jax.ShapeDtypeStruct((attn_w, n), BF16),
            jax.ShapeDtypeStruct((n, conv_w), F32),
            jax.ShapeDtypeStruct((n, LANES), F32),
        ],
        compiler_params=pltpu.CompilerParams(
            dimension_semantics=("arbitrary",), vmem_limit_bytes=VMEM_LIMIT),
        name="in_proj",
    )(x2, g, wqv_t, wrest, bf)


def _cumsum_kernel(lf_ref, tri_ref, scat_ref, ones_ref, kaug_ref, crow_ref, *, blk, heads):
    seq = lf_ref.shape[1]
    nblk = seq // blk
    tri = tri_ref[...]
    local = []
    for r in range(nblk):
        parts = _split3(lf_ref[0, r * blk:(r + 1) * blk, :])
        local.append(sum(jnp.dot(tri, p, preferred_element_type=F32) for p in parts))
    offset = jnp.zeros((1, LANES), F32)
    for r in range(nblk):
        c = local[r] + offset
        offset = c[blk - 1:blk, :]
        aug = ones_ref[...] + sum(
            jnp.dot(p, scat_ref[t], preferred_element_type=F32)
            for t, p in enumerate(_split3(-c)))
        kaug_ref[0, r * blk:(r + 1) * blk, :] = aug.astype(BF16)
        crow_ref[0, :, r * blk:(r + 1) * blk] = c.T[:heads, :]


def _forget_cumsum(lf3, heads, *, blk):
    b, s, _ = lf3.shape
    tri = (jnp.arange(blk)[:, None] >= jnp.arange(blk)[None, :]).astype(BF16)
    lane = jnp.arange(LANES)
    scat = jnp.stack([
        ((lane[None, :] == N_SPLIT + N_SPLIT * lane[:, None] + t) & (lane[:, None] < heads)).astype(BF16)
        for t in range(N_SPLIT)])
    ones_row = (lane < N_SPLIT).astype(F32).reshape(1, LANES)
    const2 = lambda i: (0, 0)
    return pl.pallas_call(
        functools.partial(_cumsum_kernel, blk=blk, heads=heads),
        grid=(b,),
        in_specs=[
            pl.BlockSpec((1, s, LANES), lambda i: (i, 0, 0)),
            pl.BlockSpec(tri.shape, const2),
            pl.BlockSpec(scat.shape, lambda i: (0, 0, 0)),
            pl.BlockSpec(ones_row.shape, const2),
        ],
        out_specs=[
            pl.BlockSpec((1, s, LANES), lambda i: (i, 0, 0)),
            pl.BlockSpec((1, heads, s), lambda i: (i, 0, 0)),
        ],
        out_shape=[
            jax.ShapeDtypeStruct((b, s, LANES), BF16),
            jax.ShapeDtypeStruct((b, heads, s), F32),
        ],
        compiler_params=pltpu.CompilerParams(dimension_semantics=("arbitrary",)),
        name="forget_cumsum",
    )(lf3, tri, scat, ones_row)


def _attn_kernel(qt_ref, k_ref, kaug_ref, vt_ref, crow_ref, g_ref, o_ref,
                 qrhs_ref, m_ref, l_ref, acc_ref, *, tq, heads):
    tk = tq
    i = pl.program_id(1)
    rowid = lax.broadcasted_iota(jnp.int32, (LANES, tq), 0)
    for h in range(heads):
        pair, e = divmod(h, 2)
        q_pair = qt_ref[pair * LANES:(pair + 1) * LANES, :]
        own = (rowid >= HEAD_DIM * e) & (rowid < HEAD_DIM * (e + 1))
        qrhs_ref[h, 0:LANES, :] = jnp.where(own, q_pair, jnp.zeros_like(q_pair))
        c_hi, c_mid, c_lo = (t.astype(F32) for t in _split3(crow_ref[0, h:h + 1, :]))
        own_lanes = (rowid >= N_SPLIT * (h + 1)) & (rowid < N_SPLIT * (h + 2))
        aug = jnp.where(rowid == 0, c_hi,
                        jnp.where(rowid == 1, c_mid,
                                  jnp.where(rowid == 2, c_lo,
                                            jnp.where(own_lanes, 1.0, 0.0))))
        qrhs_ref[h, LANES:2 * LANES, :] = aug.astype(BF16)
    m_ref[...] = jnp.full(m_ref.shape, NEG_BIG, F32)
    l_ref[...] = jnp.zeros(l_ref.shape, F32)
    acc_ref[...] = jnp.zeros(acc_ref.shape, F32)

    key_pos = lax.broadcasted_iota(jnp.int32, (tk, tq), 0)
    qry_pos = lax.broadcasted_iota(jnp.int32, (tk, tq), 1)
    causal = key_pos <= qry_pos

    def update(j, masked):
        k0 = pl.multiple_of(j * tk, tk)
        kaug = kaug_ref[0, pl.ds(k0, tk), :]

        def scores(h):
            pair = h // 2
            k_lhs = jnp.concatenate(
                [k_ref[0, pl.ds(k0, tk), pair * LANES:(pair + 1) * LANES], kaug], axis=1)
            return jnp.dot(k_lhs, qrhs_ref[h], preferred_element_type=F32)

        s_all = [scores(h) for h in range(heads)]
        for h in range(heads):
            pair, e = divmod(h, 2)
            s_t = s_all[h]
            if masked:
                s_t = jnp.where(causal, s_t, NEG_BIG)
            m_old = m_ref[h:h + 1, :]
            m_new = jnp.maximum(m_old, jnp.max(s_t, axis=0, keepdims=True))
            alpha = jnp.exp(m_old - m_new)
            p = jnp.exp(s_t - m_new)
            l_ref[h:h + 1, :] = alpha * l_ref[h:h + 1, :] + jnp.sum(p, axis=0, keepdims=True)
            v_t = vt_ref[pair * LANES:(pair + 1) * LANES, pl.ds(k0, tk)]
            pv = jnp.dot(v_t, p.astype(BF16), preferred_element_type=F32)
            rows = slice(h * HEAD_DIM, (h + 1) * HEAD_DIM)
            acc_ref[rows, :] = alpha * acc_ref[rows, :] + pv[e * HEAD_DIM:(e + 1) * HEAD_DIM, :]
            m_ref[h:h + 1, :] = m_new

    def body(j, carry):
        update(j, False)
        return carry

    lax.fori_loop(0, i, body, 0)
    update(i, True)

    for h in range(heads):
        rows = slice(h * HEAD_DIM, (h + 1) * HEAD_DIM)
        o = acc_ref[rows, :] / l_ref[h:h + 1, :]
        ms = jnp.mean(o * o, axis=0, keepdims=True)
        o_ref[rows, :] = (o * lax.rsqrt(ms + EPS) * g_ref[rows, :]).astype(o_ref.dtype)


def _attention(q_t, k3, kaug, v_t, c_row, g_col, *, tq):
    b, s, w = k3.shape
    heads = w // HEAD_DIM
    nq = s // tq
    return pl.pallas_call(
        functools.partial(_attn_kernel, tq=tq, heads=heads),
        grid=(b, nq),
        in_specs=[
            pl.BlockSpec((w, tq), lambda bi, i: (0, bi * nq + i)),
            pl.BlockSpec((1, s, w), lambda bi, i: (bi, 0, 0)),
            pl.BlockSpec((1, s, LANES), lambda bi, i: (bi, 0, 0)),
            pl.BlockSpec((w, s), lambda bi, i: (0, bi)),
            pl.BlockSpec((1, heads, tq), lambda bi, i: (bi, 0, i)),
            pl.BlockSpec((w, 1), lambda bi, i: (0, 0)),
        ],
        out_specs=pl.BlockSpec((w, tq), lambda bi, i: (0, bi * nq + i)),
        out_shape=jax.ShapeDtypeStruct((w, b * s), BF16),
        scratch_shapes=[
            pltpu.VMEM((heads, 2 * LANES, tq), BF16),
            pltpu.VMEM((heads, tq), F32),
            pltpu.VMEM((heads, tq), F32),
            pltpu.VMEM((w, tq), F32),
        ],
        compiler_params=pltpu.CompilerParams(
            dimension_semantics=("arbitrary", "arbitrary"), vmem_limit_bytes=VMEM_LIMIT),
        name="fox_attention",
    )(q_t, k3, kaug, v_t, c_row, g_col)


def _depthwise_strips(u_ref, halo_ref, dw_ref, db_ref, pad_ref, dwout_ref, *, first_in_seq, sub_rows):
    tm, width = u_ref.shape
    halo = halo_ref.shape[0]
    prev = halo_ref[...]
    pad_ref[0:halo, :] = jnp.where(first_in_seq, jnp.zeros_like(prev), prev)
    pad_ref[halo:, :] = u_ref[...]
    lead = halo - (CONV_KERNEL - 1)
    window = sub_rows + halo

    def strip(sb, cb):
        lanes = slice(cb * LANES, (cb + 1) * LANES)
        x_win = pad_ref[sb * sub_rows:sb * sub_rows + window, lanes]
        acc = jnp.zeros((sub_rows, LANES), F32) + db_ref[:, lanes]
        for res in range(SUBLANES):
            taps = [t for t in range(CONV_KERNEL) if (lead + t) % SUBLANES == res]
            shifted = x_win if res == 0 else pltpu.roll(x_win, window - res, axis=0)
            for t in taps:
                a0 = (lead + t) - res
                acc = acc + shifted[a0:a0 + sub_rows, :] * dw_ref[t:t + 1, lanes]
        dwout_ref[sb * sub_rows:(sb + 1) * sub_rows, lanes] = acc
        return jnp.sum(acc, axis=0, keepdims=True)

    return [functools.partial(strip, sb, cb)
            for sb in range(tm // sub_rows) for cb in range(width // LANES)]


def _norm_swish(x, lg_ref, lb_ref):
    mu = jnp.mean(x, axis=-1, keepdims=True)
    xc = x - mu
    var = jnp.mean(xc * xc, axis=-1, keepdims=True)
    y = xc * lax.rsqrt(var + EPS) * lg_ref[...] + lb_ref[...]
    return y * (1.0 / (1.0 + jnp.exp(-y)))


def _post_kernel(x_ref, at_ref, u_ref, halo_ref, dw_ref, db_ref, lg_ref, lb_ref, pw_ref, pb_ref,
                 og_ref, gsum_ref, woa_ref, woc_ref, gf_ref, wup_ref, wdn_ref, gl_ref,
                 o_ref, pad_ref, dwout_ref, y_ref,
                 *, tiles_per_seq, n_tiles, sub_rows, ff_chunks, apply_final_norm):
    g = pl.program_id(0)

    @pl.when(g == 0)
    def _():
        y_ref[...] = jnp.zeros(y_ref.shape, y_ref.dtype)

    z = jnp.dot(y_ref[...], pw_ref[...], preferred_element_type=F32) + pb_ref[...]
    ms = _split_dot(z * z, gsum_ref[...]) * (1.0 / HEAD_DIM)
    conv = (z * lax.rsqrt(ms + EPS) * og_ref[...]).astype(BF16)
    x1 = (x_ref[...]
          + lax.dot_general(at_ref[...], woa_ref[...], TN_DIMS, preferred_element_type=F32)
          + jnp.dot(conv, woc_ref[...], preferred_element_type=F32))
    h = _rms(x1, gf_ref[...]).astype(BF16)
    o_ref[...] = x1

    tile = jnp.minimum(g, n_tiles - 1)
    first_in_seq = lax.rem(tile, tiles_per_seq) == 0
    strips = _depthwise_strips(u_ref, halo_ref, dw_ref, db_ref, pad_ref, dwout_ref,
                               first_in_seq=first_in_seq, sub_rows=sub_rows)
    d_ff = wup_ref.shape[1]
    ff = d_ff // ff_chunks
    per_chunk = len(strips) // ff_chunks
    assert per_chunk == ff // LANES
    for c in range(ff_chunks):
        cols = slice(c * ff, (c + 1) * ff)
        tokens = [strip() for strip in strips[c * per_chunk:(c + 1) * per_chunk]]
        floor = jnp.concatenate([jnp.where(g >= 0, 0.0, t) for t in tokens], axis=1)
        a = jnp.maximum(jnp.dot(h, wup_ref[:, cols], preferred_element_type=F32), floor)
        o_ref[...] += jnp.dot((a * a).astype(BF16), wdn_ref[cols, :], preferred_element_type=F32)
    if apply_final_norm:
        o_ref[...] = _rms(o_ref[...], gl_ref[...])
    y_ref[...] = _norm_swish(dwout_ref[...], lg_ref, lb_ref).astype(y_ref.dtype)


def _post(x2, attn_t, u, dw, db, lg, lb, pw, pb, og, gsum, woa, woc, gf, wup, wdn, gl,
          *, tm, seq, apply_final_norm):
    n, d = x2.shape
    w = u.shape[1]
    halo = 32
    n_tiles = n // tm
    const = lambda g: (0, 0)
    prev_tile = lambda g: (jnp.maximum(g - 1, 0), 0)
    cur_tile = lambda g: (jnp.minimum(g, n_tiles - 1), 0)
    halo_blk = lambda g: (jnp.maximum(jnp.minimum(g, n_tiles - 1) * (tm // halo) - 1, 0), 0)
    vec = pl.BlockSpec((1, w), const)
    return pl.pallas_call(
        functools.partial(_post_kernel, tiles_per_seq=seq // tm, n_tiles=n_tiles, sub_rows=tm // 8,
                          ff_chunks=4,
                          apply_final_norm=apply_final_norm),
        grid=(n_tiles + 1,),
        in_specs=[
            pl.BlockSpec((tm, d), prev_tile),
            pl.BlockSpec((attn_t.shape[0], tm), lambda g: (0, jnp.maximum(g - 1, 0))),
            pl.BlockSpec((tm, w), cur_tile),
            pl.BlockSpec((halo, w), halo_blk),
            pl.BlockSpec(dw.shape, const),
            vec, vec, vec,
            pl.BlockSpec(pw.shape, const),
            vec, vec,
            pl.BlockSpec(gsum.shape, const),
            pl.BlockSpec(woa.shape, const),
            pl.BlockSpec(woc.shape, const),
            pl.BlockSpec((1, d), const),
            pl.BlockSpec(wup.shape, const),
            pl.BlockSpec(wdn.shape, const),
            pl.BlockSpec((1, d), const),
        ],
        out_specs=pl.BlockSpec((tm, d), prev_tile),
        out_shape=jax.ShapeDtypeStruct((n, d), F32),
        scratch_shapes=[
            pltpu.VMEM((tm + halo, w), F32),
            pltpu.VMEM((tm, w), F32),
            pltpu.VMEM((tm, w), BF16),
        ],
        compiler_params=pltpu.CompilerParams(
            dimension_semantics=("arbitrary",), vmem_limit_bytes=VMEM_LIMIT),
        name="conv_out_proj_ffn",
    )(x2, attn_t, u, u, dw, db, lg, lb, pw, pb, og, gsum, woa, woc, gf, wup, wdn, gl)


def _group_sum_matrix(width):
    idx = jnp.arange(width) // HEAD_DIM
    return (idx[:, None] == idx[None, :]).astype(BF16)


def kernel(x, norm_mix_g, w_in, b_forget, conv_dw_w, conv_dw_b, conv_ln_g, conv_ln_b,
           w_conv_pw, b_conv_pw, attn_out_g, conv_out_g, w_out, norm_ffn_g,
           w_ffn_up, w_ffn_down, norm_final_g):
    b, s, d = x.shape
    depth = w_in.shape[0]
    heads = b_forget.shape[1]
    attn_w = heads * HEAD_DIM
    conv_w = conv_dw_w.shape[2]
    n = b * s
    x2 = x.reshape(n, d)
    gsum_conv = _group_sum_matrix(conv_w)
    for l in range(depth):
        w = w_in[l]
        wq, wk, wv = w[:, :attn_w], w[:, attn_w:2 * attn_w], w[:, 2 * attn_w:3 * attn_w]
        wqv_t = jnp.concatenate([wq, wv], axis=1).T.astype(BF16)
        wf = jnp.pad(w[:, 3 * attn_w + 2 * conv_w:], ((0, 0), (0, LANES - heads)))
        wrest = jnp.concatenate(
            [wk, w[:, 3 * attn_w:3 * attn_w + 2 * conv_w], wf], axis=1).astype(BF16)
        bfg = jnp.pad(b_forget[l], (0, LANES - heads)).reshape(1, LANES)
        q_t, k, v_t, u, lf = _in_proj(x2, norm_mix_g[l].reshape(1, d), wqv_t, wrest, bfg,
                                      attn_w=attn_w, conv_w=conv_w, tm=512)

        kaug, c_row = _forget_cumsum(lf.reshape(b, s, LANES), heads, blk=256)
        attn_t = _attention(q_t, k.reshape(b, s, attn_w), kaug, v_t, c_row,
                            attn_out_g[l].reshape(attn_w, 1), tq=256)

        vec = lambda a: a.reshape(1, conv_w)
        wo = w_out[l].astype(BF16)
        x2 = _post(x2, attn_t, u, conv_dw_w[l], vec(conv_dw_b[l]), vec(conv_ln_g[l]),
                   vec(conv_ln_b[l]), w_conv_pw[l].astype(BF16), vec(b_conv_pw[l]),
                   vec(conv_out_g[l]), gsum_conv, wo[:attn_w], wo[attn_w:],
                   norm_ffn_g[l].reshape(1, d), w_ffn_up[l].astype(BF16),
                   w_ffn_down[l].astype(BF16), norm_final_g.reshape(1, d),
                   tm=256, seq=s, apply_final_norm=(l == depth - 1))
    return x2.reshape(b, s, d)
```

```python
import functools

import jax
import jax.numpy as jnp
from jax import lax
from jax.experimental import pallas as pl
from jax.experimental.pallas import tpu as pltpu

F32 = jnp.float32
BF16 = jnp.bfloat16

HEAD_DIM = 64
CONV_KERNEL = 31
EPS = 1e-6
LANES = 128
SUBLANES = 8
VMEM_LIMIT = 56 * 1024 * 1024
NEG_BIG = -1e30
LOG2E = 1.4426950408889634
N_SPLIT = 3
NT_DIMS = (((1,), (1,)), ((), ()))
TN_DIMS = (((0,), (0,)), ((), ()))


def _rms(x, g):
    ms = jnp.mean(x * x, axis=-1, keepdims=True)
    return x * lax.rsqrt(ms + EPS) * g


def _split3(x):
    hi = x.astype(BF16)
    r = x - hi.astype(F32)
    mid = r.astype(BF16)
    lo = (r - mid.astype(F32)).astype(BF16)
    return hi, mid, lo


def _split_dot(a, b_bf16):
    hi = a.astype(BF16)
    lo = (a - hi.astype(F32)).astype(BF16)
    return (jnp.dot(hi, b_bf16, preferred_element_type=F32)
            + jnp.dot(lo, b_bf16, preferred_element_type=F32))


def _in_proj_kernel(x_ref, g_ref, wqv_t_ref, wrest_ref, bf_ref,
                    qt_ref, k_ref, vt_ref, u_ref, lf_ref, *, attn_w, conv_w):
    h = _rms(x_ref[...], g_ref[...]).astype(BF16)
    qv_t = lax.dot_general(wqv_t_ref[...], h, NT_DIMS, preferred_element_type=F32)
    qt_ref[...] = (qv_t[:attn_w] * (LOG2E * HEAD_DIM ** -0.5)).astype(BF16)
    vt_ref[...] = qv_t[attn_w:].astype(BF16)
    rest = jnp.dot(h, wrest_ref[...], preferred_element_type=F32)
    k_ref[...] = rest[:, :attn_w].astype(BF16)
    ga = rest[:, attn_w:attn_w + conv_w]
    gb = rest[:, attn_w + conv_w:attn_w + 2 * conv_w]
    u_ref[...] = ga * (1.0 / (1.0 + jnp.exp(-gb)))
    fl = rest[:, attn_w + 2 * conv_w:] + bf_ref[...]
    lf_ref[...] = jnp.minimum(fl, 0.0) - jnp.log1p(jnp.exp(-jnp.abs(fl)))


def _in_proj(x2, g, wqv_t, wrest, bf, *, attn_w, conv_w, tm):
    n, d = x2.shape
    const = lambda i: (0, 0)
    row = lambda i: (i, 0)
    col = lambda i: (0, i)
    return pl.pallas_call(
        functools.partial(_in_proj_kernel, attn_w=attn_w, conv_w=conv_w),
        grid=(n // tm,),
        in_specs=[
            pl.BlockSpec((tm, d), row),
            pl.BlockSpec((1, d), const),
            pl.BlockSpec(wqv_t.shape, const),
            pl.BlockSpec(wrest.shape, const),
            pl.BlockSpec(bf.shape, const),
        ],
        out_specs=[
            pl.BlockSpec((attn_w, tm), col),
            pl.BlockSpec((tm, attn_w), row),
            pl.BlockSpec((attn_w, tm), col),
            pl.BlockSpec((tm, conv_w), row),
            pl.BlockSpec((tm, LANES), row),
        ],
        out_shape=[
            jax.ShapeDtypeStruct((attn_w, n), BF16),
            jax.ShapeDtypeStruct((n, attn_w), BF16),
            jax.ShapeDtypeStruct((attn_w, n), BF16),
            jax.ShapeDtypeStruct((n, conv_w), F32),
            jax.ShapeDtypeStruct((n, LANES), F32),
        ],
        compiler_params=pltpu.CompilerParams(
            dimension_semantics=("arbitrary",), vmem_limit_bytes=VMEM_LIMIT),
        name="in_proj",
    )(x2, g, wqv_t, wrest, bf)


def _cumsum_kernel(lf_ref, tri_ref, scat_ref, ones_ref, kaug_ref, crow_ref, *, blk, heads):
    seq = lf_ref.shape[1]
    nblk = seq // blk
    tri = tri_ref[...]
    local = []
    for r in range(nblk):
        parts = _split3(lf_ref[0, r * blk:(r + 1) * blk, :])
        local.append(sum(jnp.dot(tri, p, preferred_element_type=F32) for p in parts))
    offset = jnp.zeros((1, LANES), F32)
    for r in range(nblk):
        c = local[r] + offset
        offset = c[blk - 1:blk, :]
        c = c * LOG2E
        aug = ones_ref[...] + sum(
            jnp.dot(p, scat_ref[t], preferred_element_type=F32)
            for t, p in enumerate(_split3(-c)))
        kaug_ref[0, r * blk:(r + 1) * blk, :] = aug.astype(BF16)
        crow_ref[0, :, r * blk:(r + 1) * blk] = c.T[:heads, :]


def _forget_cumsum(lf3, heads, *, blk):
    b, s, _ = lf3.shape
    tri = (jnp.arange(blk)[:, None] >= jnp.arange(blk)[None, :]).astype(BF16)
    lane = jnp.arange(LANES)
    scat = jnp.stack([
        ((lane[None, :] == N_SPLIT + N_SPLIT * lane[:, None] + t) & (lane[:, None] < heads)).astype(BF16)
        for t in range(N_SPLIT)])
    ones_row = (lane < N_SPLIT).astype(F32).reshape(1, LANES)
    const2 = lambda i: (0, 0)
    return pl.pallas_call(
        functools.partial(_cumsum_kernel, blk=blk, heads=heads),
        grid=(b,),
        in_specs=[
            pl.BlockSpec((1, s, LANES), lambda i: (i, 0, 0)),
            pl.BlockSpec(tri.shape, const2),
            pl.BlockSpec(scat.shape, lambda i: (0, 0, 0)),
            pl.BlockSpec(ones_row.shape, const2),
        ],
        out_specs=[
            pl.BlockSpec((1, s, LANES), lambda i: (i, 0, 0)),
            pl.BlockSpec((1, heads, s), lambda i: (i, 0, 0)),
        ],
        out_shape=[
            jax.ShapeDtypeStruct((b, s, LANES), BF16),
            jax.ShapeDtypeStruct((b, heads, s), F32),
        ],
        compiler_params=pltpu.CompilerParams(dimension_semantics=("arbitrary",)),
        name="forget_cumsum",
    )(lf3, tri, scat, ones_row)


def _allmax_sublanes(x):
    for shift in (4, 2, 1):
        x = jnp.maximum(x, pltpu.roll(x, shift, axis=0))
    return x


def _attn_kernel(qt_ref, k_ref, kaug_ref, vt_ref, crow_ref, g_ref, o_ref,
                 qrhs_ref, m_ref, l_ref, acc_ref, *, tq, heads):
    tk = tq
    n_q = k_ref.shape[1] // tq
    rowid = lax.broadcasted_iota(jnp.int32, (LANES, tq), 0)
    key_pos = lax.broadcasted_iota(jnp.int32, (tk, tq), 0)
    qry_pos = lax.broadcasted_iota(jnp.int32, (tk, tq), 1)
    causal = key_pos <= qry_pos

    def prepare(i, slot):
        qcols = slice(i * tq, (i + 1) * tq)
        for h in range(heads):
            pair, e = divmod(h, 2)
            q_pair = qt_ref[pair * LANES:(pair + 1) * LANES, qcols]
            own = (rowid >= HEAD_DIM * e) & (rowid < HEAD_DIM * (e + 1))
            qrhs_ref[slot, h, 0:LANES, :] = jnp.where(own, q_pair, jnp.zeros_like(q_pair))
            c_hi, c_mid, c_lo = (t.astype(F32) for t in _split3(crow_ref[0, h:h + 1, qcols]))
            own_lanes = (rowid >= N_SPLIT * (h + 1)) & (rowid < N_SPLIT * (h + 2))
            aug = jnp.where(rowid == 0, c_hi,
                            jnp.where(rowid == 1, c_mid,
                                      jnp.where(rowid == 2, c_lo,
                                                jnp.where(own_lanes, 1.0, 0.0))))
            qrhs_ref[slot, h, LANES:2 * LANES, :] = aug.astype(BF16)
        m_ref[slot] = jnp.full(m_ref.shape[1:], NEG_BIG, F32)
        l_ref[slot] = jnp.zeros(l_ref.shape[1:], F32)
        acc_ref[slot] = jnp.zeros(acc_ref.shape[1:], F32)

    def update(slot, j, masked):
        keys = slice(j * tk, (j + 1) * tk)
        kaug = kaug_ref[0, keys, :]

        def scores(h):
            pair = h // 2
            k_lhs = jnp.concatenate(
                [k_ref[0, keys, pair * LANES:(pair + 1) * LANES], kaug], axis=1)
            return jnp.dot(k_lhs, qrhs_ref[slot, h], preferred_element_type=F32)

        s_all = [scores(h) for h in range(heads)]
        ones_rows = jnp.ones((2 * SUBLANES, tk), BF16)
        for h in range(heads):
            s_t = s_all[h]
            if masked:
                s_t = jnp.where(causal, s_t, NEG_BIG)
            s3 = s_t.reshape(tk // SUBLANES, SUBLANES, tq)
            m_old = m_ref[slot, h]
            m_new = jnp.maximum(m_old, _allmax_sublanes(jnp.max(s3, axis=0)))
            alpha = jnp.exp2(m_old - m_new)
            p = jnp.exp2(s3 - m_new[None]).reshape(tk, tq).astype(BF16)
            rows = slice(h * HEAD_DIM, (h + 1) * HEAD_DIM)
            v_aug = jnp.concatenate([vt_ref[rows, keys], ones_rows], axis=0)
            pv = jnp.dot(v_aug, p, preferred_element_type=F32)
            l_ref[slot, h] = alpha * l_ref[slot, h] + pv[HEAD_DIM:HEAD_DIM + SUBLANES, :]
            acc3 = acc_ref[slot, rows, :].reshape(HEAD_DIM // SUBLANES, SUBLANES, tq)
            pv3 = pv[:HEAD_DIM, :].reshape(HEAD_DIM // SUBLANES, SUBLANES, tq)
            acc_ref[slot, rows, :] = (alpha[None] * acc3 + pv3).reshape(HEAD_DIM, tq)
            m_ref[slot, h] = m_new

    def finish(i, slot):
        qcols = slice(i * tq, (i + 1) * tq)
        for h in range(heads):
            rows = slice(h * HEAD_DIM, (h + 1) * HEAD_DIM)
            acc3 = acc_ref[slot, rows, :].reshape(HEAD_DIM // SUBLANES, SUBLANES, tq)
            o = (acc3 / l_ref[slot, h][None]).reshape(HEAD_DIM, tq)
            ms = jnp.mean(o * o, axis=0, keepdims=True)
            o_ref[rows, qcols] = (o * lax.rsqrt(ms + EPS) * g_ref[rows, :]).astype(o_ref.dtype)

    for i in range(n_q):
        slot = i % 2
        prepare(i, slot)
        for j in range(i + 1):
            update(slot, j, masked=(j == i))
        finish(i, slot)


def _attention(q_t, k3, kaug, v_t, c_row, g_col, *, tq):
    b, s, w = k3.shape
    heads = w // HEAD_DIM
    return pl.pallas_call(
        functools.partial(_attn_kernel, tq=tq, heads=heads),
        grid=(b,),
        in_specs=[
            pl.BlockSpec((w, s), lambda bi: (0, bi)),
            pl.BlockSpec((1, s, w), lambda bi: (bi, 0, 0)),
            pl.BlockSpec((1, s, LANES), lambda bi: (bi, 0, 0)),
            pl.BlockSpec((w, s), lambda bi: (0, bi)),
            pl.BlockSpec((1, heads, s), lambda bi: (bi, 0, 0)),
            pl.BlockSpec((w, 1), lambda bi: (0, 0)),
        ],
        out_specs=pl.BlockSpec((w, s), lambda bi: (0, bi)),
        out_shape=jax.ShapeDtypeStruct((w, b * s), BF16),
        scratch_shapes=[
            pltpu.VMEM((2, heads, 2 * LANES, tq), BF16),
            pltpu.VMEM((2, heads, SUBLANES, tq), F32),
            pltpu.VMEM((2, heads, SUBLANES, tq), F32),
            pltpu.VMEM((2, w, tq), F32),
        ],
        compiler_params=pltpu.CompilerParams(
            dimension_semantics=("arbitrary",), vmem_limit_bytes=VMEM_LIMIT),
        name="fox_attention",
    )(q_t, k3, kaug, v_t, c_row, g_col)


def _depthwise_strips(u_ref, halo_ref, dw_ref, db_ref, pad_ref, dwout_ref, *, first_in_seq, sub_rows):
    tm, width = u_ref.shape
    halo = halo_ref.shape[0]
    prev = halo_ref[...]
    pad_ref[0:halo, :] = jnp.where(first_in_seq, jnp.zeros_like(prev), prev)
    pad_ref[halo:, :] = u_ref[...]
    lead = halo - (CONV_KERNEL - 1)
    window = sub_rows + halo

    def strip(sb, cb):
        lanes = slice(cb * LANES, (cb + 1) * LANES)
        x_win = pad_ref[sb * sub_rows:sb * sub_rows + window, lanes]
        acc = jnp.zeros((sub_rows, LANES), F32) + db_ref[:, lanes]
        for res in range(SUBLANES):
            taps = [t for t in range(CONV_KERNEL) if (lead + t) % SUBLANES == res]
            shifted = x_win if res == 0 else pltpu.roll(x_win, window - res, axis=0)
            for t in taps:
                a0 = (lead + t) - res
                acc = acc + shifted[a0:a0 + sub_rows, :] * dw_ref[t:t + 1, lanes]
        dwout_ref[sb * sub_rows:(sb + 1) * sub_rows, lanes] = acc
        return jnp.sum(acc, axis=0, keepdims=True)

    return [functools.partial(strip, sb, cb)
            for sb in range(tm // sub_rows) for cb in range(width // LANES)]


def _norm_swish(x, lg_ref, lb_ref):
    mu = jnp.mean(x, axis=-1, keepdims=True)
    xc = x - mu
    var = jnp.mean(xc * xc, axis=-1, keepdims=True)
    y = xc * lax.rsqrt(var + EPS) * lg_ref[...] + lb_ref[...]
    return y * (1.0 / (1.0 + jnp.exp(-y)))


def _post_kernel(x_ref, at_ref, u_ref, halo_ref, dw_ref, db_ref, lg_ref, lb_ref, pw_ref, pb_ref,
                 og_ref, gsum_ref, woa_ref, woc_ref, gf_ref, wup_ref, wdn_ref, gl_ref,
                 o_ref, pad_ref, dwout_ref, y_ref,
                 *, tiles_per_seq, n_tiles, sub_rows, ff_chunks, apply_final_norm):
    g = pl.program_id(0)

    @pl.when(g == 0)
    def _():
        y_ref[...] = jnp.zeros(y_ref.shape, y_ref.dtype)

    z = jnp.dot(y_ref[...], pw_ref[...], preferred_element_type=F32) + pb_ref[...]
    ms = _split_dot(z * z, gsum_ref[...]) * (1.0 / HEAD_DIM)
    conv = (z * lax.rsqrt(ms + EPS) * og_ref[...]).astype(BF16)
    x1 = (x_ref[...]
          + lax.dot_general(at_ref[...], woa_ref[...], TN_DIMS, preferred_element_type=F32)
          + jnp.dot(conv, woc_ref[...], preferred_element_type=F32))
    h = _rms(x1, gf_ref[...]).astype(BF16)
    o_ref[...] = x1

    tile = jnp.minimum(g, n_tiles - 1)
    first_in_seq = lax.rem(tile, tiles_per_seq) == 0
    strips = _depthwise_strips(u_ref, halo_ref, dw_ref, db_ref, pad_ref, dwout_ref,
                               first_in_seq=first_in_seq, sub_rows=sub_rows)
    d_ff = wup_ref.shape[1]
    ff = d_ff // ff_chunks
    per_chunk = len(strips) // ff_chunks
    assert per_chunk == ff // LANES
    for c in range(ff_chunks):
        cols = slice(c * ff, (c + 1) * ff)
        tokens = [strip() for strip in strips[c * per_chunk:(c + 1) * per_chunk]]
        floor = jnp.concatenate([jnp.where(g >= 0, 0.0, t) for t in tokens], axis=1)
        a = jnp.maximum(jnp.dot(h, wup_ref[:, cols], preferred_element_type=F32), floor)
        o_ref[...] += jnp.dot((a * a).astype(BF16), wdn_ref[cols, :], preferred_element_type=F32)
    if apply_final_norm:
        o_ref[...] = _rms(o_ref[...], gl_ref[...])
    y_ref[...] = _norm_swish(dwout_ref[...], lg_ref, lb_ref).astype(y_ref.dtype)


def _post(x2, attn_t, u, dw, db, lg, lb, pw, pb, og, gsum, woa, woc, gf, wup, wdn, gl,
          *, tm, seq, apply_final_norm):
    n, d = x2.shape
    w = u.shape[1]
    halo = 32
    n_tiles = n // tm
    const = lambda g: (0, 0)
    prev_tile = lambda g: (jnp.maximum(g - 1, 0), 0)
    cur_tile = lambda g: (jnp.minimum(g, n_tiles - 1), 0)
    halo_blk = lambda g: (jnp.maximum(jnp.minimum(g, n_tiles - 1) * (tm // halo) - 1, 0), 0)
    vec = pl.BlockSpec((1, w), const)
    return pl.pallas_call(
        functools.partial(_post_kernel, tiles_per_seq=seq // tm, n_tiles=n_tiles, sub_rows=tm // 8,
                          ff_chunks=4,
                          apply_final_norm=apply_final_norm),
        grid=(n_tiles + 1,),
        in_specs=[
            pl.BlockSpec((tm, d), prev_tile),
            pl.BlockSpec((attn_t.shape[0], tm), lambda g: (0, jnp.maximum(g - 1, 0))),
            pl.BlockSpec((tm, w), cur_tile),
            pl.BlockSpec((halo, w), halo_blk),
            pl.BlockSpec(dw.shape, const),
            vec, vec, vec,
            pl.BlockSpec(pw.shape, const),
            vec, vec,
            pl.BlockSpec(gsum.shape, const),
            pl.BlockSpec(woa.shape, const),
            pl.BlockSpec(woc.shape, const),
            pl.BlockSpec((1, d), const),
            pl.BlockSpec(wup.shape, const),
            pl.BlockSpec(wdn.shape, const),
            pl.BlockSpec((1, d), const),
        ],
        out_specs=pl.BlockSpec((tm, d), prev_tile),
        out_shape=jax.ShapeDtypeStruct((n, d), F32),
        scratch_shapes=[
            pltpu.VMEM((tm + halo, w), F32),
            pltpu.VMEM((tm, w), F32),
            pltpu.VMEM((tm, w), BF16),
        ],
        compiler_params=pltpu.CompilerParams(
            dimension_semantics=("arbitrary",), vmem_limit_bytes=VMEM_LIMIT),
        name="conv_out_proj_ffn",
    )(x2, attn_t, u, u, dw, db, lg, lb, pw, pb, og, gsum, woa, woc, gf, wup, wdn, gl)


def _group_sum_matrix(width):
    idx = jnp.arange(width) // HEAD_DIM
    return (idx[:, None] == idx[None, :]).astype(BF16)


def kernel(x, norm_mix_g, w_in, b_forget, conv_dw_w, conv_dw_b, conv_ln_g, conv_ln_b,
           w_conv_pw, b_conv_pw, attn_out_g, conv_out_g, w_out, norm_ffn_g,
           w_ffn_up, w_ffn_down, norm_final_g):
    b, s, d = x.shape
    depth = w_in.shape[0]
    heads = b_forget.shape[1]
    attn_w = heads * HEAD_DIM
    conv_w = conv_dw_w.shape[2]
    n = b * s
    x2 = x.reshape(n, d)
    gsum_conv = _group_sum_matrix(conv_w)
    for l in range(depth):
        w = w_in[l]
        wq, wk, wv = w[:, :attn_w], w[:, attn_w:2 * attn_w], w[:, 2 * attn_w:3 * attn_w]
        wqv_t = jnp.concatenate([wq, wv], axis=1).T.astype(BF16)
        wf = jnp.pad(w[:, 3 * attn_w + 2 * conv_w:], ((0, 0), (0, LANES - heads)))
        wrest = jnp.concatenate(
            [wk, w[:, 3 * attn_w:3 * attn_w + 2 * conv_w], wf], axis=1).astype(BF16)
        bfg = jnp.pad(b_forget[l], (0, LANES - heads)).reshape(1, LANES)
        q_t, k, v_t, u, lf = _in_proj(x2, norm_mix_g[l].reshape(1, d), wqv_t, wrest, bfg,
                                      attn_w=attn_w, conv_w=conv_w, tm=512)

        kaug, c_row = _forget_cumsum(lf.reshape(b, s, LANES), heads, blk=256)
        attn_t = _attention(q_t, k.reshape(b, s, attn_w), kaug, v_t, c_row,
                            attn_out_g[l].reshape(attn_w, 1), tq=256)

        vec = lambda a: a.reshape(1, conv_w)
        wo = w_out[l].astype(BF16)
        x2 = _post(x2, attn_t, u, conv_dw_w[l], vec(conv_dw_b[l]), vec(conv_ln_g[l]),
                   vec(conv_ln_b[l]), w_conv_pw[l].astype(BF16), vec(b_conv_pw[l]),
                   vec(conv_out_g[l]), gsum_conv, wo[:attn_w], wo[attn_w:],
                   norm_ffn_g[l].reshape(1, d), w_ffn_up[l].astype(BF16),
                   w_ffn_down[l].astype(BF16), norm_final_g.reshape(1, d),
                   tm=256, seq=s, apply_final_norm=(l == depth - 1))
    return x2.reshape(b, s, d)
```

```python
import functools

import jax
import jax.numpy as jnp
from jax import lax
from jax.experimental import pallas as pl
from jax.experimental.pallas import tpu as pltpu

F32 = jnp.float32
BF16 = jnp.bfloat16

HEAD_DIM = 64
CONV_KERNEL = 31
EPS = 1e-6
LANES = 128
SUBLANES = 8
VMEM_LIMIT = 56 * 1024 * 1024
NEG_BIG = -1e30
LOG2E = 1.4426950408889634
N_SPLIT = 3
NT_DIMS = (((1,), (1,)), ((), ()))
TN_DIMS = (((0,), (0,)), ((), ()))


def _rms(x, g):
    ms = jnp.mean(x * x, axis=-1, keepdims=True)
    return x * lax.rsqrt(ms + EPS) * g


def _split3(x):
    hi = x.astype(BF16)
    r = x - hi.astype(F32)
    mid = r.astype(BF16)
    lo = (r - mid.astype(F32)).astype(BF16)
    return hi, mid, lo


def _split_dot(a, b_bf16):
    hi = a.astype(BF16)
    lo = (a - hi.astype(F32)).astype(BF16)
    return (jnp.dot(hi, b_bf16, preferred_element_type=F32)
            + jnp.dot(lo, b_bf16, preferred_element_type=F32))


def _in_proj_kernel(x_ref, g_ref, wqv_t_ref, wrest_ref, bf_ref,
                    qt_ref, k_ref, vt_ref, u_ref, lf_ref, *, attn_w, conv_w):
    h = _rms(x_ref[...], g_ref[...]).astype(BF16)
    qv_t = lax.dot_general(wqv_t_ref[...], h, NT_DIMS, preferred_element_type=F32)
    qt_ref[...] = (qv_t[:attn_w] * (LOG2E * HEAD_DIM ** -0.5)).astype(BF16)
    vt_ref[...] = qv_t[attn_w:].astype(BF16)
    rest = jnp.dot(h, wrest_ref[...], preferred_element_type=F32)
    k_ref[...] = rest[:, :attn_w].astype(BF16)
    ga = rest[:, attn_w:attn_w + conv_w]
    gb = rest[:, attn_w + conv_w:attn_w + 2 * conv_w]
    u_ref[...] = ga * (1.0 / (1.0 + jnp.exp(-gb)))
    fl = rest[:, attn_w + 2 * conv_w:] + bf_ref[...]
    lf_ref[...] = jnp.minimum(fl, 0.0) - jnp.log1p(jnp.exp(-jnp.abs(fl)))


def _in_proj(x2, g, wqv_t, wrest, bf, *, attn_w, conv_w, tm):
    n, d = x2.shape
    const = lambda i: (0, 0)
    row = lambda i: (i, 0)
    col = lambda i: (0, i)
    return pl.pallas_call(
        functools.partial(_in_proj_kernel, attn_w=attn_w, conv_w=conv_w),
        grid=(n // tm,),
        in_specs=[
            pl.BlockSpec((tm, d), row),
            pl.BlockSpec((1, d), const),
            pl.BlockSpec(wqv_t.shape, const),
            pl.BlockSpec(wrest.shape, const),
            pl.BlockSpec(bf.shape, const),
        ],
        out_specs=[
            pl.BlockSpec((attn_w, tm), col),
            pl.BlockSpec((tm, attn_w), row),
            pl.BlockSpec((attn_w, tm), col),
            pl.BlockSpec((tm, conv_w), row),
            pl.BlockSpec((tm, LANES), row),
        ],
        out_shape=[
            jax.ShapeDtypeStruct((attn_w, n), BF16),
            jax.ShapeDtypeStruct((n, attn_w), BF16),
            jax.ShapeDtypeStruct((attn_w, n), BF16),
            jax.ShapeDtypeStruct((n, conv_w), F32),
            jax.ShapeDtypeStruct((n, LANES), F32),
        ],
        compiler_params=pltpu.CompilerParams(
            dimension_semantics=("arbitrary",), vmem_limit_bytes=VMEM_LIMIT),
        name="in_proj",
    )(x2, g, wqv_t, wrest, bf)


def _cumsum_kernel(lf_ref, tri_ref, scat_ref, ones_ref, kaug_ref, crow_ref, *, blk, heads):
    seq = lf_ref.shape[1]
    nblk = seq // blk
    tri = tri_ref[...]
    local = []
    for r in range(nblk):
        parts = _split3(lf_ref[0, r * blk:(r + 1) * blk, :])
        local.append(sum(jnp.dot(tri, p, preferred_element_type=F32) for p in parts))
    offset = jnp.zeros((1, LANES), F32)
    for r in range(nblk):
        c = local[r] + offset
        offset = c[blk - 1:blk, :]
        c = c * LOG2E
        aug = ones_ref[...] + sum(
            jnp.dot(p, scat_ref[t], preferred_element_type=F32)
            for t, p in enumerate(_split3(-c)))
        kaug_ref[0, r * blk:(r + 1) * blk, :] = aug.astype(BF16)
        crow_ref[0, :, r * blk:(r + 1) * blk] = c.T[:heads, :]


def _forget_cumsum(lf3, heads, *, blk):
    b, s, _ = lf3.shape
    tri = (jnp.arange(blk)[:, None] >= jnp.arange(blk)[None, :]).astype(BF16)
    lane = jnp.arange(LANES)
    scat = jnp.stack([
        ((lane[None, :] == N_SPLIT + N_SPLIT * lane[:, None] + t) & (lane[:, None] < heads)).astype(BF16)
        for t in range(N_SPLIT)])
    ones_row = (lane < N_SPLIT).astype(F32).reshape(1, LANES)
    const2 = lambda i: (0, 0)
    return pl.pallas_call(
        functools.partial(_cumsum_kernel, blk=blk, heads=heads),
        grid=(b,),
        in_specs=[
            pl.BlockSpec((1, s, LANES), lambda i: (i, 0, 0)),
            pl.BlockSpec(tri.shape, const2),
            pl.BlockSpec(scat.shape, lambda i: (0, 0, 0)),
            pl.BlockSpec(ones_row.shape, const2),
        ],
        out_specs=[
            pl.BlockSpec((1, s, LANES), lambda i: (i, 0, 0)),
            pl.BlockSpec((1, heads, s), lambda i: (i, 0, 0)),
        ],
        out_shape=[
            jax.ShapeDtypeStruct((b, s, LANES), BF16),
            jax.ShapeDtypeStruct((b, heads, s), F32),
        ],
        compiler_params=pltpu.CompilerParams(dimension_semantics=("arbitrary",)),
        name="forget_cumsum",
    )(lf3, tri, scat, ones_row)


def _allmax_sublanes(x):
    for shift in (4, 2, 1):
        x = jnp.maximum(x, pltpu.roll(x, shift, axis=0))
    return x


def _attn_kernel(qt_ref, k_ref, kaug_ref, vt_ref, crow_ref, g_ref, o_ref,
                 qrhs_ref, m_ref, l_ref, acc_ref, *, tq, heads):
    tk = tq
    n_q = k_ref.shape[1] // tq
    rowid = lax.broadcasted_iota(jnp.int32, (LANES, tq), 0)
    key_pos = lax.broadcasted_iota(jnp.int32, (tk, tq), 0)
    qry_pos = lax.broadcasted_iota(jnp.int32, (tk, tq), 1)
    causal = key_pos <= qry_pos

    def prepare(i, slot):
        qcols = slice(i * tq, (i + 1) * tq)
        for h in range(heads):
            pair, e = divmod(h, 2)
            q_pair = qt_ref[pair * LANES:(pair + 1) * LANES, qcols]
            own = (rowid >= HEAD_DIM * e) & (rowid < HEAD_DIM * (e + 1))
            qrhs_ref[slot, h, 0:LANES, :] = jnp.where(own, q_pair, jnp.zeros_like(q_pair))
            c_hi, c_mid, c_lo = (t.astype(F32) for t in _split3(crow_ref[0, h:h + 1, qcols]))
            own_lanes = (rowid >= N_SPLIT * (h + 1)) & (rowid < N_SPLIT * (h + 2))
            aug = jnp.where(rowid == 0, c_hi,
                            jnp.where(rowid == 1, c_mid,
                                      jnp.where(rowid == 2, c_lo,
                                                jnp.where(own_lanes, 1.0, 0.0))))
            qrhs_ref[slot, h, LANES:2 * LANES, :] = aug.astype(BF16)
        m_ref[slot] = jnp.full(m_ref.shape[1:], NEG_BIG, F32)
        l_ref[slot] = jnp.zeros(l_ref.shape[1:], F32)
        acc_ref[slot] = jnp.zeros(acc_ref.shape[1:], F32)

    def update(slot, j, masked):
        keys = slice(j * tk, (j + 1) * tk)
        kaug = kaug_ref[0, keys, :]

        def scores(h):
            pair = h // 2
            k_lhs = jnp.concatenate(
                [k_ref[0, keys, pair * LANES:(pair + 1) * LANES], kaug], axis=1)
            return jnp.dot(k_lhs, qrhs_ref[slot, h], preferred_element_type=F32)

        s_all = [scores(h) for h in range(heads)]
        ones_rows = jnp.ones((2 * SUBLANES, tk), BF16)
        for h in range(heads):
            s_t = s_all[h]
            if masked:
                s_t = jnp.where(causal, s_t, NEG_BIG)
            s3 = s_t.reshape(tk // SUBLANES, SUBLANES, tq)
            m_old = m_ref[slot, h]
            m_new = jnp.maximum(m_old, _allmax_sublanes(jnp.max(s3, axis=0)))
            alpha = jnp.exp2(m_old - m_new)
            p = jnp.exp2(s3 - m_new[None]).reshape(tk, tq).astype(BF16)
            rows = slice(h * HEAD_DIM, (h + 1) * HEAD_DIM)
            v_aug = jnp.concatenate([vt_ref[rows, keys], ones_rows], axis=0)
            pv = jnp.dot(v_aug, p, preferred_element_type=F32)
            l_ref[slot, h] = alpha * l_ref[slot, h] + pv[HEAD_DIM:HEAD_DIM + SUBLANES, :]
            acc3 = acc_ref[slot, rows, :].reshape(HEAD_DIM // SUBLANES, SUBLANES, tq)
            pv3 = pv[:HEAD_DIM, :].reshape(HEAD_DIM // SUBLANES, SUBLANES, tq)
            acc_ref[slot, rows, :] = (alpha[None] * acc3 + pv3).reshape(HEAD_DIM, tq)
            m_ref[slot, h] = m_new

    def finish(i, slot):
        qcols = slice(i * tq, (i + 1) * tq)
        for h in range(heads):
            rows = slice(h * HEAD_DIM, (h + 1) * HEAD_DIM)
            acc3 = acc_ref[slot, rows, :].reshape(HEAD_DIM // SUBLANES, SUBLANES, tq)
            o = (acc3 / l_ref[slot, h][None]).reshape(HEAD_DIM, tq)
            ms = jnp.mean(o * o, axis=0, keepdims=True)
            o_ref[rows, qcols] = (o * lax.rsqrt(ms + EPS) * g_ref[rows, :]).astype(o_ref.dtype)

    for i in range(n_q):
        slot = i % 2
        prepare(i, slot)
        for j in range(i + 1):
            update(slot, j, masked=(j == i))
        finish(i, slot)


def _attention(q_t, k3, kaug, v_t, c_row, g_col, *, tq):
    b, s, w = k3.shape
    heads = w // HEAD_DIM
    return pl.pallas_call(
        functools.partial(_attn_kernel, tq=tq, heads=heads),
        grid=(b,),
        in_specs=[
            pl.BlockSpec((w, s), lambda bi: (0, bi)),
            pl.BlockSpec((1, s, w), lambda bi: (bi, 0, 0)),
            pl.BlockSpec((1, s, LANES), lambda bi: (bi, 0, 0)),
            pl.BlockSpec((w, s), lambda bi: (0, bi)),
            pl.BlockSpec((1, heads, s), lambda bi: (bi, 0, 0)),
            pl.BlockSpec((w, 1), lambda bi: (0, 0)),
        ],
        out_specs=pl.BlockSpec((w, s), lambda bi: (0, bi)),
        out_shape=jax.ShapeDtypeStruct((w, b * s), BF16),
        scratch_shapes=[
            pltpu.VMEM((2, heads, 2 * LANES, tq), BF16),
            pltpu.VMEM((2, heads, SUBLANES, tq), F32),
            pltpu.VMEM((2, heads, SUBLANES, tq), F32),
            pltpu.VMEM((2, w, tq), F32),
        ],
        compiler_params=pltpu.CompilerParams(
            dimension_semantics=("arbitrary",), vmem_limit_bytes=VMEM_LIMIT),
        name="fox_attention",
    )(q_t, k3, kaug, v_t, c_row, g_col)


def _depthwise_strips(u_ref, halo_ref, dw_ref, db_ref, pad_ref, dwout_ref, *, first_in_seq, sub_rows):
    tm, width = u_ref.shape
    halo = halo_ref.shape[0]
    prev = halo_ref[...]
    pad_ref[0:halo, :] = jnp.where(first_in_seq, jnp.zeros_like(prev), prev)
    pad_ref[halo:, :] = u_ref[...]
    lead = halo - (CONV_KERNEL - 1)
    window = sub_rows + halo

    def strip(sb, cb):
        lanes = slice(cb * LANES, (cb + 1) * LANES)
        x_win = pad_ref[sb * sub_rows:sb * sub_rows + window, lanes]
        acc = jnp.zeros((sub_rows, LANES), F32) + db_ref[:, lanes]
        for res in range(SUBLANES):
            taps = [t for t in range(CONV_KERNEL) if (lead + t) % SUBLANES == res]
            shifted = x_win if res == 0 else pltpu.roll(x_win, window - res, axis=0)
            for t in taps:
                a0 = (lead + t) - res
                acc = acc + shifted[a0:a0 + sub_rows, :] * dw_ref[t:t + 1, lanes]
        dwout_ref[sb * sub_rows:(sb + 1) * sub_rows, lanes] = acc
        return jnp.sum(acc, axis=0, keepdims=True)

    return [functools.partial(strip, sb, cb)
            for sb in range(tm // sub_rows) for cb in range(width // LANES)]


def _norm_swish(x, lg_ref, lb_ref):
    mu = jnp.mean(x, axis=-1, keepdims=True)
    xc = x - mu
    var = jnp.mean(xc * xc, axis=-1, keepdims=True)
    y = xc * lax.rsqrt(var + EPS) * lg_ref[...] + lb_ref[...]
    return y * (1.0 / (1.0 + jnp.exp(-y)))


def _post_kernel(x_ref, at_ref, u_ref, halo_ref, dw_ref, db_ref, lg_ref, lb_ref, pw_ref, pb_ref,
                 og_ref, gsum_ref, woa_ref, woc_ref, gf_ref, wup_ref, wdn_ref, gl_ref,
                 o_ref, pad_ref, dwout_ref, y_ref,
                 *, tiles_per_seq, n_tiles, sub_rows, ff_chunks, apply_final_norm):
    g = pl.program_id(0)

    @pl.when(g == 0)
    def _():
        y_ref[...] = jnp.zeros(y_ref.shape, y_ref.dtype)

    tile = jnp.minimum(g, n_tiles - 1)
    first_in_seq = lax.rem(tile, tiles_per_seq) == 0
    strips = _depthwise_strips(u_ref, halo_ref, dw_ref, db_ref, pad_ref, dwout_ref,
                               first_in_seq=first_in_seq, sub_rows=sub_rows)
    tm = o_ref.shape[0]
    ff = wup_ref.shape[1] // ff_chunks
    head_strips = 5
    per_chunk = (len(strips) - head_strips) // (ff_chunks - 1)
    assert head_strips + per_chunk * (ff_chunks - 1) == len(strips)
    norm_rows = tm // 4
    top = 2 * SUBLANES

    def norm_block(r):
        rows = slice(r * norm_rows, (r + 1) * norm_rows)
        y = _norm_swish(dwout_ref[rows, :], lg_ref, lb_ref)
        y_ref[rows, :] = y.astype(y_ref.dtype)
        t = jnp.sum(y, axis=0, keepdims=True)
        return sum(t[:, i * LANES:(i + 1) * LANES] for i in range(y.shape[1] // LANES))

    def tied(lhs, work):
        tokens = [item() for item in work]
        t = sum(tokens[1:], tokens[0])
        corner = jnp.where(g >= 0, lhs[:top, :LANES], jnp.broadcast_to(t, (top, LANES)).astype(BF16))
        return jnp.concatenate(
            [jnp.concatenate([corner, lhs[:top, LANES:]], axis=1), lhs[top:, :]], axis=0)

    z = jnp.dot(y_ref[...], pw_ref[...], preferred_element_type=F32) + pb_ref[...]
    x_attn = x_ref[...] + lax.dot_general(at_ref[...], woa_ref[...], TN_DIMS,
                                          preferred_element_type=F32)
    ms = _split_dot(z * z, gsum_ref[...]) * (1.0 / HEAD_DIM)
    conv = (z * lax.rsqrt(ms + EPS) * og_ref[...]).astype(BF16)
    conv = tied(conv, strips[:head_strips])
    x1 = x_attn + jnp.dot(conv, woc_ref[...], preferred_element_type=F32)
    h = _rms(x1, gf_ref[...]).astype(BF16)
    o_ref[...] = x1
    for c in range(ff_chunks):
        cols = slice(c * ff, (c + 1) * ff)
        lo = head_strips + (c - 1) * per_chunk
        h_c = tied(h, strips[lo:lo + per_chunk]) if c else h
        a = jnp.maximum(jnp.dot(h_c, wup_ref[:, cols], preferred_element_type=F32), 0.0)
        a2 = (a * a).astype(BF16)
        if c == ff_chunks - 1:
            a2 = tied(a2, [functools.partial(norm_block, r) for r in range(tm // norm_rows)])
        o_ref[...] += jnp.dot(a2, wdn_ref[cols, :], preferred_element_type=F32)
    if apply_final_norm:
        o_ref[...] = _rms(o_ref[...], gl_ref[...])


def _post(x2, attn_t, u, dw, db, lg, lb, pw, pb, og, gsum, woa, woc, gf, wup, wdn, gl,
          *, tm, seq, apply_final_norm):
    n, d = x2.shape
    w = u.shape[1]
    halo = 32
    n_tiles = n // tm
    const = lambda g: (0, 0)
    prev_tile = lambda g: (jnp.maximum(g - 1, 0), 0)
    cur_tile = lambda g: (jnp.minimum(g, n_tiles - 1), 0)
    halo_blk = lambda g: (jnp.maximum(jnp.minimum(g, n_tiles - 1) * (tm // halo) - 1, 0), 0)
    vec = pl.BlockSpec((1, w), const)
    resident = lambda a: pl.BlockSpec(a.shape, const, pipeline_mode=pl.Buffered(1))
    return pl.pallas_call(
        functools.partial(_post_kernel, tiles_per_seq=seq // tm, n_tiles=n_tiles, sub_rows=tm // 8,
                          ff_chunks=4,
                          apply_final_norm=apply_final_norm),
        grid=(n_tiles + 1,),
        in_specs=[
            pl.BlockSpec((tm, d), prev_tile),
            pl.BlockSpec((attn_t.shape[0], tm), lambda g: (0, jnp.maximum(g - 1, 0))),
            pl.BlockSpec((tm, w), cur_tile),
            pl.BlockSpec((halo, w), halo_blk),
            pl.BlockSpec(dw.shape, const),
            vec, vec, vec,
            pl.BlockSpec(pw.shape, const),
            vec, vec,
            pl.BlockSpec(gsum.shape, const),
            resident(woa),
            resident(woc),
            pl.BlockSpec((1, d), const),
            resident(wup),
            resident(wdn),
            pl.BlockSpec((1, d), const),
        ],
        out_specs=pl.BlockSpec((tm, d), prev_tile),
        out_shape=jax.ShapeDtypeStruct((n, d), F32),
        scratch_shapes=[
            pltpu.VMEM((tm + halo, w), F32),
            pltpu.VMEM((tm, w), F32),
            pltpu.VMEM((tm, w), BF16),
        ],
        compiler_params=pltpu.CompilerParams(
            dimension_semantics=("arbitrary",), vmem_limit_bytes=VMEM_LIMIT),
        name="conv_out_proj_ffn",
    )(x2, attn_t, u, u, dw, db, lg, lb, pw, pb, og, gsum, woa, woc, gf, wup, wdn, gl)


def _group_sum_matrix(width):
    idx = jnp.arange(width) // HEAD_DIM
    return (idx[:, None] == idx[None, :]).astype(BF16)


def kernel(x, norm_mix_g, w_in, b_forget, conv_dw_w, conv_dw_b, conv_ln_g, conv_ln_b,
           w_conv_pw, b_conv_pw, attn_out_g, conv_out_g, w_out, norm_ffn_g,
           w_ffn_up, w_ffn_down, norm_final_g):
    b, s, d = x.shape
    depth = w_in.shape[0]
    heads = b_forget.shape[1]
    attn_w = heads * HEAD_DIM
    conv_w = conv_dw_w.shape[2]
    n = b * s
    x2 = x.reshape(n, d)
    gsum_conv = _group_sum_matrix(conv_w)
    for l in range(depth):
        w = w_in[l]
        wq, wk, wv = w[:, :attn_w], w[:, attn_w:2 * attn_w], w[:, 2 * attn_w:3 * attn_w]
        wqv_t = jnp.concatenate([wq, wv], axis=1).T.astype(BF16)
        wf = jnp.pad(w[:, 3 * attn_w + 2 * conv_w:], ((0, 0), (0, LANES - heads)))
        wrest = jnp.concatenate(
            [wk, w[:, 3 * attn_w:3 * attn_w + 2 * conv_w], wf], axis=1).astype(BF16)
        bfg = jnp.pad(b_forget[l], (0, LANES - heads)).reshape(1, LANES)
        q_t, k, v_t, u, lf = _in_proj(x2, norm_mix_g[l].reshape(1, d), wqv_t, wrest, bfg,
                                      attn_w=attn_w, conv_w=conv_w, tm=512)

        kaug, c_row = _forget_cumsum(lf.reshape(b, s, LANES), heads, blk=256)
        attn_t = _attention(q_t, k.reshape(b, s, attn_w), kaug, v_t, c_row,
                            attn_out_g[l].reshape(attn_w, 1), tq=256)

        vec = lambda a: a.reshape(1, conv_w)
        wo = w_out[l].astype(BF16)
        x2 = _post(x2, attn_t, u, conv_dw_w[l], vec(conv_dw_b[l]), vec(conv_ln_g[l]),
                   vec(conv_ln_b[l]), w_conv_pw[l].astype(BF16), vec(b_conv_pw[l]),
                   vec(conv_out_g[l]), gsum_conv, wo[:attn_w], wo[attn_w:],
                   norm_ffn_g[l].reshape(1, d), w_ffn_up[l].astype(BF16),
                   w_ffn_down[l].astype(BF16), norm_final_g.reshape(1, d),
                   tm=512, seq=s, apply_final_norm=(l == depth - 1))
    return x2.reshape(b, s, d)
```

```python
import functools

import jax
import jax.numpy as jnp
from jax import lax
from jax.experimental import pallas as pl
from jax.experimental.pallas import tpu as pltpu

F32 = jnp.float32
BF16 = jnp.bfloat16

HEAD_DIM = 64
CONV_KERNEL = 31
EPS = 1e-6
LANES = 128
SUBLANES = 8
VMEM_LIMIT = 56 * 1024 * 1024
NEG_BIG = -1e30
LOG2E = 1.4426950408889634
N_SPLIT = 3
NT_DIMS = (((1,), (1,)), ((), ()))
TN_DIMS = (((0,), (0,)), ((), ()))


def _rms(x, g):
    ms = jnp.mean(x * x, axis=-1, keepdims=True)
    return x * lax.rsqrt(ms + EPS) * g


def _split3(x):
    hi = x.astype(BF16)
    r = x - hi.astype(F32)
    mid = r.astype(BF16)
    lo = (r - mid.astype(F32)).astype(BF16)
    return hi, mid, lo


def _split_dot(a, b_bf16):
    hi = a.astype(BF16)
    lo = (a - hi.astype(F32)).astype(BF16)
    return (jnp.dot(hi, b_bf16, preferred_element_type=F32)
            + jnp.dot(lo, b_bf16, preferred_element_type=F32))


def _in_proj_kernel(x_ref, g_ref, wqv_t_ref, wrest_ref, bf_ref,
                    qt_ref, k_ref, vt_ref, u_ref, lf_ref, *, attn_w, conv_w):
    tm = x_ref.shape[0]
    halves = [slice(i * (tm // 2), (i + 1) * (tm // 2)) for i in range(2)]
    hs = [_rms(x_ref[r, :], g_ref[...]).astype(BF16) for r in halves]
    for r, h in zip(halves, hs):
        qv_t = lax.dot_general(wqv_t_ref[...], h, NT_DIMS, preferred_element_type=F32)
        qt_ref[:, r] = (qv_t[:attn_w] * (LOG2E * HEAD_DIM ** -0.5)).astype(BF16)
        vt_ref[:, r] = qv_t[attn_w:].astype(BF16)
    for r, h in zip(halves, hs):
        rest = jnp.dot(h, wrest_ref[...], preferred_element_type=F32)
        k_ref[r, :] = rest[:, :attn_w].astype(BF16)
        ga = rest[:, attn_w:attn_w + conv_w]
        gb = rest[:, attn_w + conv_w:attn_w + 2 * conv_w]
        u_ref[r, :] = ga * (1.0 / (1.0 + jnp.exp(-gb)))
        fl = rest[:, attn_w + 2 * conv_w:] + bf_ref[...]
        lf_ref[r, :] = jnp.minimum(fl, 0.0) - jnp.log1p(jnp.exp(-jnp.abs(fl)))


def _in_proj(x2, g, wqv_t, wrest, bf, *, attn_w, conv_w, tm):
    n, d = x2.shape
    const = lambda i: (0, 0)
    row = lambda i: (i, 0)
    col = lambda i: (0, i)
    return pl.pallas_call(
        functools.partial(_in_proj_kernel, attn_w=attn_w, conv_w=conv_w),
        grid=(n // tm,),
        in_specs=[
            pl.BlockSpec((tm, d), row),
            pl.BlockSpec((1, d), const),
            pl.BlockSpec(wqv_t.shape, const),
            pl.BlockSpec(wrest.shape, const),
            pl.BlockSpec(bf.shape, const),
        ],
        out_specs=[
            pl.BlockSpec((attn_w, tm), col),
            pl.BlockSpec((tm, attn_w), row),
            pl.BlockSpec((attn_w, tm), col),
            pl.BlockSpec((tm, conv_w), row),
            pl.BlockSpec((tm, LANES), row),
        ],
        out_shape=[
            jax.ShapeDtypeStruct((attn_w, n), BF16),
            jax.ShapeDtypeStruct((n, attn_w), BF16),
            jax.ShapeDtypeStruct((attn_w, n), BF16),
            jax.ShapeDtypeStruct((n, conv_w), F32),
            jax.ShapeDtypeStruct((n, LANES), F32),
        ],
        compiler_params=pltpu.CompilerParams(
            dimension_semantics=("arbitrary",), vmem_limit_bytes=VMEM_LIMIT),
        name="in_proj",
    )(x2, g, wqv_t, wrest, bf)


def _cumsum_kernel(lf_ref, tri_ref, scat_ref, ones_ref, kaug_ref, crow_ref, *, blk, heads):
    seq = lf_ref.shape[1]
    nblk = seq // blk
    tri = tri_ref[...]
    local = []
    for r in range(nblk):
        parts = _split3(lf_ref[0, r * blk:(r + 1) * blk, :])
        local.append(sum(jnp.dot(tri, p, preferred_element_type=F32) for p in parts))
    offset = jnp.zeros((1, LANES), F32)
    for r in range(nblk):
        c = local[r] + offset
        offset = c[blk - 1:blk, :]
        c = c * LOG2E
        aug = ones_ref[...] + sum(
            jnp.dot(p, scat_ref[t], preferred_element_type=F32)
            for t, p in enumerate(_split3(-c)))
        kaug_ref[0, r * blk:(r + 1) * blk, :] = aug.astype(BF16)
        crow_ref[0, :, r * blk:(r + 1) * blk] = c.T[:heads, :]


def _forget_cumsum(lf3, heads, *, blk):
    b, s, _ = lf3.shape
    tri = (jnp.arange(blk)[:, None] >= jnp.arange(blk)[None, :]).astype(BF16)
    lane = jnp.arange(LANES)
    scat = jnp.stack([
        ((lane[None, :] == N_SPLIT + N_SPLIT * lane[:, None] + t) & (lane[:, None] < heads)).astype(BF16)
        for t in range(N_SPLIT)])
    ones_row = (lane < N_SPLIT).astype(F32).reshape(1, LANES)
    const2 = lambda i: (0, 0)
    return pl.pallas_call(
        functools.partial(_cumsum_kernel, blk=blk, heads=heads),
        grid=(b,),
        in_specs=[
            pl.BlockSpec((1, s, LANES), lambda i: (i, 0, 0)),
            pl.BlockSpec(tri.shape, const2),
            pl.BlockSpec(scat.shape, lambda i: (0, 0, 0)),
            pl.BlockSpec(ones_row.shape, const2),
        ],
        out_specs=[
            pl.BlockSpec((1, s, LANES), lambda i: (i, 0, 0)),
            pl.BlockSpec((1, heads, s), lambda i: (i, 0, 0)),
        ],
        out_shape=[
            jax.ShapeDtypeStruct((b, s, LANES), BF16),
            jax.ShapeDtypeStruct((b, heads, s), F32),
        ],
        compiler_params=pltpu.CompilerParams(dimension_semantics=("arbitrary",)),
        name="forget_cumsum",
    )(lf3, tri, scat, ones_row)


def _allmax_sublanes(x):
    for shift in (4, 2, 1):
        x = jnp.maximum(x, pltpu.roll(x, shift, axis=0))
    return x


def _attn_kernel(qt_ref, k_ref, kaug_ref, vt_ref, crow_ref, g_ref, o_ref,
                 qrhs_ref, m_ref, l_ref, acc_ref, *, tq, heads):
    tk = tq
    n_q = k_ref.shape[1] // tq
    rowid = lax.broadcasted_iota(jnp.int32, (LANES, tq), 0)
    key_pos = lax.broadcasted_iota(jnp.int32, (tk, tq), 0)
    qry_pos = lax.broadcasted_iota(jnp.int32, (tk, tq), 1)
    causal = key_pos <= qry_pos

    def prepare(i, slot):
        qcols = slice(i * tq, (i + 1) * tq)
        for h in range(heads):
            pair, e = divmod(h, 2)
            q_pair = qt_ref[pair * LANES:(pair + 1) * LANES, qcols]
            own = (rowid >= HEAD_DIM * e) & (rowid < HEAD_DIM * (e + 1))
            qrhs_ref[slot, h, 0:LANES, :] = jnp.where(own, q_pair, jnp.zeros_like(q_pair))
            c_hi, c_mid, c_lo = (t.astype(F32) for t in _split3(crow_ref[0, h:h + 1, qcols]))
            own_lanes = (rowid >= N_SPLIT * (h + 1)) & (rowid < N_SPLIT * (h + 2))
            aug = jnp.where(rowid == 0, c_hi,
                            jnp.where(rowid == 1, c_mid,
                                      jnp.where(rowid == 2, c_lo,
                                                jnp.where(own_lanes, 1.0, 0.0))))
            qrhs_ref[slot, h, LANES:2 * LANES, :] = aug.astype(BF16)
        m_ref[slot] = jnp.full(m_ref.shape[1:], NEG_BIG, F32)
        l_ref[slot] = jnp.zeros(l_ref.shape[1:], F32)
        acc_ref[slot] = jnp.zeros(acc_ref.shape[1:], F32)

    ones_rows = jnp.ones((2 * SUBLANES, tk), BF16)

    def scores(i, j, h):
        keys = slice(j * tk, (j + 1) * tk)
        pair = h // 2
        k_lhs = jnp.concatenate(
            [k_ref[0, keys, pair * LANES:(pair + 1) * LANES], kaug_ref[0, keys, :]], axis=1)
        return jnp.dot(k_lhs, qrhs_ref[i % 2, h], preferred_element_type=F32)

    def absorb(i, j, h, s_t):
        slot = i % 2
        keys = slice(j * tk, (j + 1) * tk)
        if j == i:
            s_t = jnp.where(causal, s_t, NEG_BIG)
        s3 = s_t.reshape(tk // SUBLANES, SUBLANES, tq)
        m_old = m_ref[slot, h]
        m_new = jnp.maximum(m_old, _allmax_sublanes(jnp.max(s3, axis=0)))
        alpha = jnp.exp2(m_old - m_new)
        p = jnp.exp2(s3 - m_new[None]).reshape(tk, tq).astype(BF16)
        rows = slice(h * HEAD_DIM, (h + 1) * HEAD_DIM)
        v_aug = jnp.concatenate([vt_ref[rows, keys], ones_rows], axis=0)
        pv = jnp.dot(v_aug, p, preferred_element_type=F32)
        l_ref[slot, h] = alpha * l_ref[slot, h] + pv[HEAD_DIM:HEAD_DIM + SUBLANES, :]
        acc3 = acc_ref[slot, rows, :].reshape(HEAD_DIM // SUBLANES, SUBLANES, tq)
        pv3 = pv[:HEAD_DIM, :].reshape(HEAD_DIM // SUBLANES, SUBLANES, tq)
        acc_ref[slot, rows, :] = (alpha[None] * acc3 + pv3).reshape(HEAD_DIM, tq)
        m_ref[slot, h] = m_new

    def finish(i, slot):
        qcols = slice(i * tq, (i + 1) * tq)
        for h in range(heads):
            rows = slice(h * HEAD_DIM, (h + 1) * HEAD_DIM)
            acc3 = acc_ref[slot, rows, :].reshape(HEAD_DIM // SUBLANES, SUBLANES, tq)
            o = (acc3 / l_ref[slot, h][None]).reshape(HEAD_DIM, tq)
            ms = jnp.mean(o * o, axis=0, keepdims=True)
            o_ref[rows, qcols] = (o * lax.rsqrt(ms + EPS) * g_ref[rows, :]).astype(o_ref.dtype)

    chains = [(i, j, h) for i in range(n_q) for j in range(i + 1) for h in range(heads)]
    lookahead = 12
    assert lookahead <= 2 * heads
    prepared = set()
    pending = {}

    def issue(p):
        if p < len(chains):
            i = chains[p][0]
            if i not in prepared:
                prepare(i, i % 2)
                prepared.add(i)
            pending[p] = scores(*chains[p])

    for p in range(lookahead):
        issue(p)
    for p, (i, j, h) in enumerate(chains):
        absorb(i, j, h, pending.pop(p))
        if j == i and h == heads - 1:
            finish(i, i % 2)
        issue(p + lookahead)


def _attention(q_t, k3, kaug, v_t, c_row, g_col, *, tq):
    b, s, w = k3.shape
    heads = w // HEAD_DIM
    return pl.pallas_call(
        functools.partial(_attn_kernel, tq=tq, heads=heads),
        grid=(b,),
        in_specs=[
            pl.BlockSpec((w, s), lambda bi: (0, bi)),
            pl.BlockSpec((1, s, w), lambda bi: (bi, 0, 0)),
            pl.BlockSpec((1, s, LANES), lambda bi: (bi, 0, 0)),
            pl.BlockSpec((w, s), lambda bi: (0, bi)),
            pl.BlockSpec((1, heads, s), lambda bi: (bi, 0, 0)),
            pl.BlockSpec((w, 1), lambda bi: (0, 0)),
        ],
        out_specs=pl.BlockSpec((w, s), lambda bi: (0, bi)),
        out_shape=jax.ShapeDtypeStruct((w, b * s), BF16),
        scratch_shapes=[
            pltpu.VMEM((2, heads, 2 * LANES, tq), BF16),
            pltpu.VMEM((2, heads, SUBLANES, tq), F32),
            pltpu.VMEM((2, heads, SUBLANES, tq), F32),
            pltpu.VMEM((2, w, tq), F32),
        ],
        compiler_params=pltpu.CompilerParams(
            dimension_semantics=("arbitrary",), vmem_limit_bytes=VMEM_LIMIT),
        name="fox_attention",
    )(q_t, k3, kaug, v_t, c_row, g_col)


def _depthwise_strips(u_ref, halo_ref, dw_ref, db_ref, pad_ref, dwout_ref, *, first_in_seq, sub_rows):
    tm, width = u_ref.shape
    halo = halo_ref.shape[0]
    prev = halo_ref[...]
    pad_ref[0:halo, :] = jnp.where(first_in_seq, jnp.zeros_like(prev), prev)
    pad_ref[halo:, :] = u_ref[...]
    lead = halo - (CONV_KERNEL - 1)
    window = sub_rows + halo

    def strip(sb, cb):
        lanes = slice(cb * LANES, (cb + 1) * LANES)
        x_win = pad_ref[sb * sub_rows:sb * sub_rows + window, lanes]
        acc = jnp.zeros((sub_rows, LANES), F32) + db_ref[:, lanes]
        for res in range(SUBLANES):
            taps = [t for t in range(CONV_KERNEL) if (lead + t) % SUBLANES == res]
            shifted = x_win if res == 0 else pltpu.roll(x_win, window - res, axis=0)
            for t in taps:
                a0 = (lead + t) - res
                acc = acc + shifted[a0:a0 + sub_rows, :] * dw_ref[t:t + 1, lanes]
        dwout_ref[sb * sub_rows:(sb + 1) * sub_rows, lanes] = acc
        return jnp.sum(acc, axis=0, keepdims=True)

    return [functools.partial(strip, sb, cb)
            for sb in range(tm // sub_rows) for cb in range(width // LANES)]


def _norm_swish(x, lg_ref, lb_ref):
    mu = jnp.mean(x, axis=-1, keepdims=True)
    xc = x - mu
    var = jnp.mean(xc * xc, axis=-1, keepdims=True)
    y = xc * lax.rsqrt(var + EPS) * lg_ref[...] + lb_ref[...]
    return y * (1.0 / (1.0 + jnp.exp(-y)))


def _post_kernel(x_ref, at_ref, u_ref, halo_ref, dw_ref, db_ref, lg_ref, lb_ref, pw_ref, pb_ref,
                 og_ref, gsum_ref, woa_ref, woc_ref, gf_ref, wup_ref, wdn_ref, gl_ref,
                 o_ref, pad_ref, dwout_ref, y_ref,
                 *, tiles_per_seq, n_tiles, sub_rows, ff_chunks, apply_final_norm):
    g = pl.program_id(0)

    @pl.when(g == 0)
    def _():
        y_ref[...] = jnp.zeros(y_ref.shape, y_ref.dtype)

    tile = jnp.minimum(g, n_tiles - 1)
    first_in_seq = lax.rem(tile, tiles_per_seq) == 0
    strips = _depthwise_strips(u_ref, halo_ref, dw_ref, db_ref, pad_ref, dwout_ref,
                               first_in_seq=first_in_seq, sub_rows=sub_rows)
    tm = o_ref.shape[0]
    ff = wup_ref.shape[1] // ff_chunks
    head_strips = 5
    per_chunk = (len(strips) - head_strips) // (ff_chunks - 1)
    assert head_strips + per_chunk * (ff_chunks - 1) == len(strips)
    norm_rows = tm // 4
    top = 2 * SUBLANES

    def norm_block(r):
        rows = slice(r * norm_rows, (r + 1) * norm_rows)
        y = _norm_swish(dwout_ref[rows, :], lg_ref, lb_ref)
        y_ref[rows, :] = y.astype(y_ref.dtype)
        t = jnp.sum(y, axis=0, keepdims=True)
        return sum(t[:, i * LANES:(i + 1) * LANES] for i in range(y.shape[1] // LANES))

    def tied(lhs, work):
        tokens = [item() for item in work]
        t = sum(tokens[1:], tokens[0])
        corner = jnp.where(g >= 0, lhs[:top, :LANES], jnp.broadcast_to(t, (top, LANES)).astype(BF16))
        return jnp.concatenate(
            [jnp.concatenate([corner, lhs[:top, LANES:]], axis=1), lhs[top:, :]], axis=0)

    z = jnp.dot(y_ref[...], pw_ref[...], preferred_element_type=F32) + pb_ref[...]
    x_attn = x_ref[...] + lax.dot_general(at_ref[...], woa_ref[...], TN_DIMS,
                                          preferred_element_type=F32)
    ms = _split_dot(z * z, gsum_ref[...]) * (1.0 / HEAD_DIM)
    conv = (z * lax.rsqrt(ms + EPS) * og_ref[...]).astype(BF16)
    conv = tied(conv, strips[:head_strips])
    x1 = x_attn + jnp.dot(conv, woc_ref[...], preferred_element_type=F32)
    h = _rms(x1, gf_ref[...]).astype(BF16)
    o_ref[...] = x1
    for c in range(ff_chunks):
        cols = slice(c * ff, (c + 1) * ff)
        lo = head_strips + (c - 1) * per_chunk
        h_c = tied(h, strips[lo:lo + per_chunk]) if c else h
        a = jnp.maximum(jnp.dot(h_c, wup_ref[:, cols], preferred_element_type=F32), 0.0)
        a2 = (a * a).astype(BF16)
        if c == ff_chunks - 1:
            a2 = tied(a2, [functools.partial(norm_block, r) for r in range(tm // norm_rows)])
        o_ref[...] += jnp.dot(a2, wdn_ref[cols, :], preferred_element_type=F32)
    if apply_final_norm:
        o_ref[...] = _rms(o_ref[...], gl_ref[...])


def _post(x2, attn_t, u, dw, db, lg, lb, pw, pb, og, gsum, woa, woc, gf, wup, wdn, gl,
          *, tm, seq, apply_final_norm):
    n, d = x2.shape
    w = u.shape[1]
    halo = 32
    n_tiles = n // tm
    const = lambda g: (0, 0)
    prev_tile = lambda g: (jnp.maximum(g - 1, 0), 0)
    cur_tile = lambda g: (jnp.minimum(g, n_tiles - 1), 0)
    halo_blk = lambda g: (jnp.maximum(jnp.minimum(g, n_tiles - 1) * (tm // halo) - 1, 0), 0)
    vec = pl.BlockSpec((1, w), const)
    resident = lambda a: pl.BlockSpec(a.shape, const, pipeline_mode=pl.Buffered(1))
    return pl.pallas_call(
        functools.partial(_post_kernel, tiles_per_seq=seq // tm, n_tiles=n_tiles, sub_rows=tm // 8,
                          ff_chunks=4,
                          apply_final_norm=apply_final_norm),
        grid=(n_tiles + 1,),
        in_specs=[
            pl.BlockSpec((tm, d), prev_tile),
            pl.BlockSpec((attn_t.shape[0], tm), lambda g: (0, jnp.maximum(g - 1, 0))),
            pl.BlockSpec((tm, w), cur_tile),
            pl.BlockSpec((halo, w), halo_blk),
            pl.BlockSpec(dw.shape, const),
            vec, vec, vec,
            pl.BlockSpec(pw.shape, const),
            vec, vec,
            pl.BlockSpec(gsum.shape, const),
            resident(woa),
            resident(woc),
            pl.BlockSpec((1, d), const),
            resident(wup),
            resident(wdn),
            pl.BlockSpec((1, d), const),
        ],
        out_specs=pl.BlockSpec((tm, d), prev_tile),
        out_shape=jax.ShapeDtypeStruct((n, d), F32),
        scratch_shapes=[
            pltpu.VMEM((tm + halo, w), F32),
            pltpu.VMEM((tm, w), F32),
            pltpu.VMEM((tm, w), BF16),
        ],
        compiler_params=pltpu.CompilerParams(
            dimension_semantics=("arbitrary",), vmem_limit_bytes=VMEM_LIMIT),
        name="conv_out_proj_ffn",
    )(x2, attn_t, u, u, dw, db, lg, lb, pw, pb, og, gsum, woa, woc, gf, wup, wdn, gl)


def _group_sum_matrix(width):
    idx = jnp.arange(width) // HEAD_DIM
    return (idx[:, None] == idx[None, :]).astype(BF16)


def kernel(x, norm_mix_g, w_in, b_forget, conv_dw_w, conv_dw_b, conv_ln_g, conv_ln_b,
           w_conv_pw, b_conv_pw, attn_out_g, conv_out_g, w_out, norm_ffn_g,
           w_ffn_up, w_ffn_down, norm_final_g):
    b, s, d = x.shape
    depth = w_in.shape[0]
    heads = b_forget.shape[1]
    attn_w = heads * HEAD_DIM
    conv_w = conv_dw_w.shape[2]
    n = b * s
    x2 = x.reshape(n, d)
    gsum_conv = _group_sum_matrix(conv_w)
    for l in range(depth):
        w = w_in[l]
        wq, wk, wv = w[:, :attn_w], w[:, attn_w:2 * attn_w], w[:, 2 * attn_w:3 * attn_w]
        wqv_t = jnp.concatenate([wq, wv], axis=1).T.astype(BF16)
        wf = jnp.pad(w[:, 3 * attn_w + 2 * conv_w:], ((0, 0), (0, LANES - heads)))
        wrest = jnp.concatenate(
            [wk, w[:, 3 * attn_w:3 * attn_w + 2 * conv_w], wf], axis=1).astype(BF16)
        bfg = jnp.pad(b_forget[l], (0, LANES - heads)).reshape(1, LANES)
        q_t, k, v_t, u, lf = _in_proj(x2, norm_mix_g[l].reshape(1, d), wqv_t, wrest, bfg,
                                      attn_w=attn_w, conv_w=conv_w, tm=1024)

        kaug, c_row = _forget_cumsum(lf.reshape(b, s, LANES), heads, blk=256)
        attn_t = _attention(q_t, k.reshape(b, s, attn_w), kaug, v_t, c_row,
                            attn_out_g[l].reshape(attn_w, 1), tq=256)

        vec = lambda a: a.reshape(1, conv_w)
        wo = w_out[l].astype(BF16)
        x2 = _post(x2, attn_t, u, conv_dw_w[l], vec(conv_dw_b[l]), vec(conv_ln_g[l]),
                   vec(conv_ln_b[l]), w_conv_pw[l].astype(BF16), vec(b_conv_pw[l]),
                   vec(conv_out_g[l]), gsum_conv, wo[:attn_w], wo[attn_w:],
                   norm_ffn_g[l].reshape(1, d), w_ffn_up[l].astype(BF16),
                   w_ffn_down[l].astype(BF16), norm_final_g.reshape(1, d),
                   tm=512, seq=s, apply_final_norm=(l == depth - 1))
    return x2.reshape(b, s, d)
```

```python
import functools

import jax
import jax.numpy as jnp
from jax import lax
from jax.experimental import pallas as pl
from jax.experimental.pallas import tpu as pltpu

F32 = jnp.float32
BF16 = jnp.bfloat16

HEAD_DIM = 64
CONV_KERNEL = 31
EPS = 1e-6
LANES = 128
SUBLANES = 8
VMEM_LIMIT = 56 * 1024 * 1024
NEG_BIG = -1e30
LOG2E = 1.4426950408889634
N_SPLIT = 3
NT_DIMS = (((1,), (1,)), ((), ()))
TN_DIMS = (((0,), (0,)), ((), ()))


def _rms(x, g):
    ms = jnp.mean(x * x, axis=-1, keepdims=True)
    return x * lax.rsqrt(ms + EPS) * g


def _split3(x):
    hi = x.astype(BF16)
    r = x - hi.astype(F32)
    mid = r.astype(BF16)
    lo = (r - mid.astype(F32)).astype(BF16)
    return hi, mid, lo


def _in_proj_kernel(x_ref, g_ref, wqv_t_ref, wrest_ref, bf_ref,
                    qt_ref, k_ref, vt_ref, u_ref, lf_ref, *, attn_w, conv_w, parts):
    tm = x_ref.shape[0]
    halves = [slice(i * (tm // parts), (i + 1) * (tm // parts)) for i in range(parts)]
    hs = [_rms(x_ref[r, :], g_ref[...]).astype(BF16) for r in halves]
    for r, h in zip(halves, hs):
        qv_t = lax.dot_general(wqv_t_ref[...], h, NT_DIMS, preferred_element_type=F32)
        qt_ref[:, r] = (qv_t[:attn_w] * (LOG2E * HEAD_DIM ** -0.5)).astype(BF16)
        vt_ref[:, r] = qv_t[attn_w:].astype(BF16)
    for r, h in zip(halves, hs):
        rest = jnp.dot(h, wrest_ref[...], preferred_element_type=F32)
        k_ref[r, :] = rest[:, :attn_w].astype(BF16)
        ga = rest[:, attn_w:attn_w + conv_w]
        gb = rest[:, attn_w + conv_w:attn_w + 2 * conv_w]
        u_ref[r, :] = ga * (1.0 / (1.0 + jnp.exp(-gb)))
        fl = rest[:, attn_w + 2 * conv_w:] + bf_ref[...]
        lf_ref[r, :] = jnp.minimum(fl, 0.0) - jnp.log1p(jnp.exp(-jnp.abs(fl)))


def _in_proj(x2, g, wqv_t, wrest, bf, *, attn_w, conv_w, tm):
    n, d = x2.shape
    const = lambda i: (0, 0)
    row = lambda i: (i, 0)
    col = lambda i: (0, i)
    return pl.pallas_call(
        functools.partial(_in_proj_kernel, attn_w=attn_w, conv_w=conv_w, parts=4),
        grid=(n // tm,),
        in_specs=[
            pl.BlockSpec((tm, d), row),
            pl.BlockSpec((1, d), const),
            pl.BlockSpec(wqv_t.shape, const),
            pl.BlockSpec(wrest.shape, const),
            pl.BlockSpec(bf.shape, const),
        ],
        out_specs=[
            pl.BlockSpec((attn_w, tm), col),
            pl.BlockSpec((tm, attn_w), row),
            pl.BlockSpec((attn_w, tm), col),
            pl.BlockSpec((tm, conv_w), row),
            pl.BlockSpec((tm, LANES), row),
        ],
        out_shape=[
            jax.ShapeDtypeStruct((attn_w, n), BF16),
            jax.ShapeDtypeStruct((n, attn_w), BF16),
            jax.ShapeDtypeStruct((attn_w, n), BF16),
            jax.ShapeDtypeStruct((n, conv_w), F32),
            jax.ShapeDtypeStruct((n, LANES), F32),
        ],
        compiler_params=pltpu.CompilerParams(
            dimension_semantics=("arbitrary",), vmem_limit_bytes=VMEM_LIMIT),
        name="in_proj",
    )(x2, g, wqv_t, wrest, bf)


def _cumsum_kernel(lf_ref, tri_ref, scat_ref, ones_ref, kaug_ref, crow_ref, *, blk, heads):
    seq = lf_ref.shape[1]
    nblk = seq // blk
    tri = tri_ref[...]
    lane = lax.broadcasted_iota(jnp.int32, (blk, LANES), 1)

    def pack3(x):
        out = jnp.zeros(x.shape, F32)
        for t, term in enumerate(_split3(x)):
            term = term.astype(F32)
            shifted = term if t == 0 else pltpu.roll(term, t * heads, axis=1)
            out = jnp.where((lane >= t * heads) & (lane < (t + 1) * heads), shifted, out)
        return out.astype(BF16)

    local = []
    for r in range(nblk):
        packed = jnp.dot(tri, pack3(lf_ref[0, r * blk:(r + 1) * blk, :]), preferred_element_type=F32)
        total = packed
        for t in range(1, N_SPLIT):
            total = total + pltpu.roll(packed, LANES - t * heads, axis=1)
        local.append(jnp.where(lane < heads, total, 0.0))
    offset = jnp.zeros((1, LANES), F32)
    for r in range(nblk):
        c = local[r] + offset
        offset = c[blk - 1:blk, :]
        c = c * LOG2E
        aug = ones_ref[...] + jnp.dot(pack3(-c), scat_ref[...], preferred_element_type=F32)
        kaug_ref[0, r * blk:(r + 1) * blk, :] = aug.astype(BF16)
        crow_ref[0, :, r * blk:(r + 1) * blk] = c.T[:heads, :]


def _forget_cumsum(lf3, heads, *, blk):
    b, s, _ = lf3.shape
    tri = (jnp.arange(blk)[:, None] >= jnp.arange(blk)[None, :]).astype(BF16)
    lane = jnp.arange(LANES)
    src_t, src_h = lane[:, None] // heads, lane[:, None] % heads
    scat = ((lane[None, :] == N_SPLIT + N_SPLIT * src_h + src_t) & (src_t < N_SPLIT)).astype(BF16)
    ones_row = (lane < N_SPLIT).astype(F32).reshape(1, LANES)
    const2 = lambda i: (0, 0)
    return pl.pallas_call(
        functools.partial(_cumsum_kernel, blk=blk, heads=heads),
        grid=(b,),
        in_specs=[
            pl.BlockSpec((1, s, LANES), lambda i: (i, 0, 0)),
            pl.BlockSpec(tri.shape, const2),
            pl.BlockSpec(scat.shape, const2),
            pl.BlockSpec(ones_row.shape, const2),
        ],
        out_specs=[
            pl.BlockSpec((1, s, LANES), lambda i: (i, 0, 0)),
            pl.BlockSpec((1, heads, s), lambda i: (i, 0, 0)),
        ],
        out_shape=[
            jax.ShapeDtypeStruct((b, s, LANES), BF16),
            jax.ShapeDtypeStruct((b, heads, s), F32),
        ],
        compiler_params=pltpu.CompilerParams(dimension_semantics=("arbitrary",)),
        name="forget_cumsum",
    )(lf3, tri, scat, ones_row)


def _allmax_sublanes(x):
    for shift in (4, 2, 1):
        x = jnp.maximum(x, pltpu.roll(x, shift, axis=0))
    return x


def _attn_kernel(qt_ref, k_ref, kaug_ref, vt_ref, crow_ref, g_ref, o_ref,
                 qrhs_ref, m_ref, l_ref, acc_ref, *, tq, heads):
    tk = tq
    n_q = k_ref.shape[1] // tq
    rowid = lax.broadcasted_iota(jnp.int32, (LANES, tq), 0)
    key_pos = lax.broadcasted_iota(jnp.int32, (tk, tq), 0)
    qry_pos = lax.broadcasted_iota(jnp.int32, (tk, tq), 1)
    causal = key_pos <= qry_pos

    def prepare(i, slot):
        qcols = slice(i * tq, (i + 1) * tq)
        for h in range(heads):
            pair, e = divmod(h, 2)
            q_pair = qt_ref[pair * LANES:(pair + 1) * LANES, qcols]
            own = (rowid >= HEAD_DIM * e) & (rowid < HEAD_DIM * (e + 1))
            qrhs_ref[slot, h, 0:LANES, :] = jnp.where(own, q_pair, jnp.zeros_like(q_pair))
            c_hi, c_mid, c_lo = (t.astype(F32) for t in _split3(crow_ref[0, h:h + 1, qcols]))
            own_lanes = (rowid >= N_SPLIT * (h + 1)) & (rowid < N_SPLIT * (h + 2))
            aug = jnp.where(rowid == 0, c_hi,
                            jnp.where(rowid == 1, c_mid,
                                      jnp.where(rowid == 2, c_lo,
                                                jnp.where(own_lanes, 1.0, 0.0))))
            qrhs_ref[slot, h, LANES:2 * LANES, :] = aug.astype(BF16)
        m_ref[slot] = jnp.full(m_ref.shape[1:], NEG_BIG, F32)
        l_ref[slot] = jnp.zeros(l_ref.shape[1:], F32)
        acc_ref[slot] = jnp.zeros(acc_ref.shape[1:], F32)

    ones_rows = jnp.ones((2 * SUBLANES, tk), BF16)

    def scores(i, j, h):
        keys = slice(j * tk, (j + 1) * tk)
        pair = h // 2
        k_lhs = jnp.concatenate(
            [k_ref[0, keys, pair * LANES:(pair + 1) * LANES], kaug_ref[0, keys, :]], axis=1)
        return jnp.dot(k_lhs, qrhs_ref[i % 2, h], preferred_element_type=F32)

    def absorb(i, j, h, s_t):
        slot = i % 2
        keys = slice(j * tk, (j + 1) * tk)
        if j == i:
            s_t = jnp.where(causal, s_t, NEG_BIG)
        s3 = s_t.reshape(tk // SUBLANES, SUBLANES, tq)
        m_old = m_ref[slot, h]
        m_new = jnp.maximum(m_old, _allmax_sublanes(jnp.max(s3, axis=0)))
        alpha = jnp.exp2(m_old - m_new)
        p = jnp.exp2(s3 - m_new[None]).reshape(tk, tq).astype(BF16)
        rows = slice(h * HEAD_DIM, (h + 1) * HEAD_DIM)
        v_aug = jnp.concatenate([vt_ref[rows, keys], ones_rows], axis=0)
        pv = jnp.dot(v_aug, p, preferred_element_type=F32)
        l_ref[slot, h] = alpha * l_ref[slot, h] + pv[HEAD_DIM:HEAD_DIM + SUBLANES, :]
        acc3 = acc_ref[slot, rows, :].reshape(HEAD_DIM // SUBLANES, SUBLANES, tq)
        pv3 = pv[:HEAD_DIM, :].reshape(HEAD_DIM // SUBLANES, SUBLANES, tq)
        acc_ref[slot, rows, :] = (alpha[None] * acc3 + pv3).reshape(HEAD_DIM, tq)
        m_ref[slot, h] = m_new

    def finish(i, slot):
        qcols = slice(i * tq, (i + 1) * tq)
        for h in range(heads):
            rows = slice(h * HEAD_DIM, (h + 1) * HEAD_DIM)
            acc3 = acc_ref[slot, rows, :].reshape(HEAD_DIM // SUBLANES, SUBLANES, tq)
            o = (acc3 / l_ref[slot, h][None]).reshape(HEAD_DIM, tq)
            ms = jnp.mean(o * o, axis=0, keepdims=True)
            o_ref[rows, qcols] = (o * lax.rsqrt(ms + EPS) * g_ref[rows, :]).astype(o_ref.dtype)

    chains = [(i, j, h) for i in range(n_q) for j in range(i + 1) for h in range(heads)]
    lookahead = 12
    assert lookahead <= 2 * heads
    prepared = set()
    pending = {}

    def issue(p):
        if p < len(chains):
            i = chains[p][0]
            if i not in prepared:
                prepare(i, i % 2)
                prepared.add(i)
            pending[p] = scores(*chains[p])

    for p in range(lookahead):
        issue(p)
    for p, (i, j, h) in enumerate(chains):
        absorb(i, j, h, pending.pop(p))
        if j == i and h == heads - 1:
            finish(i, i % 2)
        issue(p + lookahead)


def _attention(q_t, k3, kaug, v_t, c_row, g_col, *, tq):
    b, s, w = k3.shape
    heads = w // HEAD_DIM
    return pl.pallas_call(
        functools.partial(_attn_kernel, tq=tq, heads=heads),
        grid=(b,),
        in_specs=[
            pl.BlockSpec((w, s), lambda bi: (0, bi)),
            pl.BlockSpec((1, s, w), lambda bi: (bi, 0, 0)),
            pl.BlockSpec((1, s, LANES), lambda bi: (bi, 0, 0)),
            pl.BlockSpec((w, s), lambda bi: (0, bi)),
            pl.BlockSpec((1, heads, s), lambda bi: (bi, 0, 0)),
            pl.BlockSpec((w, 1), lambda bi: (0, 0)),
        ],
        out_specs=pl.BlockSpec((w, s), lambda bi: (0, bi)),
        out_shape=jax.ShapeDtypeStruct((w, b * s), BF16),
        scratch_shapes=[
            pltpu.VMEM((2, heads, 2 * LANES, tq), BF16),
            pltpu.VMEM((2, heads, SUBLANES, tq), F32),
            pltpu.VMEM((2, heads, SUBLANES, tq), F32),
            pltpu.VMEM((2, w, tq), F32),
        ],
        compiler_params=pltpu.CompilerParams(
            dimension_semantics=("arbitrary",), vmem_limit_bytes=VMEM_LIMIT),
        name="fox_attention",
    )(q_t, k3, kaug, v_t, c_row, g_col)


def _depthwise_strips(u_ref, halo_ref, dw_ref, db_ref, pad_ref, dwout_ref, *, first_in_seq, sub_rows):
    tm, width = u_ref.shape
    halo = halo_ref.shape[0]
    prev = halo_ref[...]
    pad_ref[0:halo, :] = jnp.where(first_in_seq, jnp.zeros_like(prev), prev)
    pad_ref[halo:, :] = u_ref[...]
    lead = halo - (CONV_KERNEL - 1)
    window = sub_rows + halo

    def strip(sb, cb):
        lanes = slice(cb * LANES, (cb + 1) * LANES)
        x_win = pad_ref[sb * sub_rows:sb * sub_rows + window, lanes]
        acc = jnp.zeros((sub_rows, LANES), F32) + db_ref[:, lanes]
        for res in range(SUBLANES):
            taps = [t for t in range(CONV_KERNEL) if (lead + t) % SUBLANES == res]
            shifted = x_win if res == 0 else pltpu.roll(x_win, window - res, axis=0)
            for t in taps:
                a0 = (lead + t) - res
                acc = acc + shifted[a0:a0 + sub_rows, :] * dw_ref[t:t + 1, lanes]
        dwout_ref[sb * sub_rows:(sb + 1) * sub_rows, lanes] = acc
        return jnp.sum(acc, axis=0, keepdims=True)

    return [functools.partial(strip, sb, cb)
            for sb in range(tm // sub_rows) for cb in range(width // LANES)]


def _norm_swish(x, lg_ref, lb_ref):
    mu = jnp.mean(x, axis=-1, keepdims=True)
    xc = x - mu
    var = jnp.mean(xc * xc, axis=-1, keepdims=True)
    y = xc * lax.rsqrt(var + EPS) * lg_ref[...] + lb_ref[...]
    return y * (1.0 / (1.0 + jnp.exp(-y)))


def _post_kernel(x_ref, at_ref, u_ref, halo_ref, dw_ref, db_ref, lg_ref, lb_ref, pw_ref, pb_ref,
                 og_ref, gsum_ref, woa_ref, woc_ref, gf_ref, wup_ref, wdn_ref, gl_ref,
                 o_ref, pad_ref, dwout_ref, y_ref,
                 *, tiles_per_seq, n_tiles, sub_rows, ff_chunks, apply_final_norm):
    g = pl.program_id(0)

    @pl.when(g == 0)
    def _():
        y_ref[...] = jnp.zeros(y_ref.shape, y_ref.dtype)

    tile = jnp.minimum(g, n_tiles - 1)
    first_in_seq = lax.rem(tile, tiles_per_seq) == 0
    strips = _depthwise_strips(u_ref, halo_ref, dw_ref, db_ref, pad_ref, dwout_ref,
                               first_in_seq=first_in_seq, sub_rows=sub_rows)
    tm = o_ref.shape[0]
    ff = wup_ref.shape[1] // ff_chunks
    bounds = [round(w * len(strips) / (ff_chunks - 1)) for w in range(ff_chunks)]
    norm_rows = tm // 4
    top = 2 * SUBLANES

    def norm_block(r):
        rows = slice(r * norm_rows, (r + 1) * norm_rows)
        y = _norm_swish(dwout_ref[rows, :], lg_ref, lb_ref)
        y_ref[rows, :] = y.astype(y_ref.dtype)
        t = jnp.sum(y, axis=0, keepdims=True)
        return sum(t[:, i * LANES:(i + 1) * LANES] for i in range(y.shape[1] // LANES))

    def tied(lhs, work):
        tokens = [item() for item in work]
        t = sum(tokens[1:], tokens[0])
        corner = jnp.where(g >= 0, lhs[:top, :LANES], jnp.broadcast_to(t, (top, LANES)).astype(BF16))
        return jnp.concatenate(
            [jnp.concatenate([corner, lhs[:top, LANES:]], axis=1), lhs[top:, :]], axis=0)

    halves = [slice(i * (tm // 2), (i + 1) * (tm // 2)) for i in range(2)]
    z = [jnp.dot(y_ref[r, :], pw_ref[...], preferred_element_type=F32) + pb_ref[...] for r in halves]
    x_attn = [x_ref[r, :] + lax.dot_general(at_ref[:, r], woa_ref[...], TN_DIMS,
                                            preferred_element_type=F32) for r in halves]
    h = []
    for i, r in enumerate(halves):
        ms = jnp.dot((z[i] * z[i]).astype(BF16), gsum_ref[...],
                     preferred_element_type=F32) * (1.0 / HEAD_DIM)
        conv = (z[i] * lax.rsqrt(ms + EPS) * og_ref[...]).astype(BF16)
        x1 = x_attn[i] + jnp.dot(conv, woc_ref[...], preferred_element_type=F32)
        o_ref[r, :] = x1
        h.append(_rms(x1, gf_ref[...]).astype(BF16))
    for c in range(ff_chunks):
        cols = slice(c * ff, (c + 1) * ff)
        lhs = [tied(h[0], strips[bounds[c - 1]:bounds[c]]) if c else h[0], h[1]]
        a2 = []
        for h_i in lhs:
            a = jnp.maximum(jnp.dot(h_i, wup_ref[:, cols], preferred_element_type=F32), 0.0)
            a2.append((a * a).astype(BF16))
        if c == ff_chunks - 1:
            a2[1] = tied(a2[1], [functools.partial(norm_block, r) for r in range(tm // norm_rows)])
        for r, a2_i in zip(halves, a2):
            o_ref[r, :] += jnp.dot(a2_i, wdn_ref[cols, :], preferred_element_type=F32)
    if apply_final_norm:
        for r in halves:
            o_ref[r, :] = _rms(o_ref[r, :], gl_ref[...])


def _post(x2, attn_t, u, dw, db, lg, lb, pw, pb, og, gsum, woa, woc, gf, wup, wdn, gl,
          *, tm, seq, apply_final_norm):
    n, d = x2.shape
    w = u.shape[1]
    halo = 32
    n_tiles = n // tm
    const = lambda g: (0, 0)
    prev_tile = lambda g: (jnp.maximum(g - 1, 0), 0)
    cur_tile = lambda g: (jnp.minimum(g, n_tiles - 1), 0)
    halo_blk = lambda g: (jnp.maximum(jnp.minimum(g, n_tiles - 1) * (tm // halo) - 1, 0), 0)
    vec = pl.BlockSpec((1, w), const)
    resident = lambda a: pl.BlockSpec(a.shape, const, pipeline_mode=pl.Buffered(1))
    return pl.pallas_call(
        functools.partial(_post_kernel, tiles_per_seq=seq // tm, n_tiles=n_tiles, sub_rows=tm // 8,
                          ff_chunks=4,
                          apply_final_norm=apply_final_norm),
        grid=(n_tiles + 1,),
        in_specs=[
            pl.BlockSpec((tm, d), prev_tile),
            pl.BlockSpec((attn_t.shape[0], tm), lambda g: (0, jnp.maximum(g - 1, 0))),
            pl.BlockSpec((tm, w), cur_tile),
            pl.BlockSpec((halo, w), halo_blk),
            pl.BlockSpec(dw.shape, const),
            vec, vec, vec,
            pl.BlockSpec(pw.shape, const),
            vec, vec,
            pl.BlockSpec(gsum.shape, const),
            resident(woa),
            resident(woc),
            pl.BlockSpec((1, d), const),
            resident(wup),
            resident(wdn),
            pl.BlockSpec((1, d), const),
        ],
        out_specs=pl.BlockSpec((tm, d), prev_tile),
        out_shape=jax.ShapeDtypeStruct((n, d), F32),
        scratch_shapes=[
            pltpu.VMEM((tm + halo, w), F32),
            pltpu.VMEM((tm, w), F32),
            pltpu.VMEM((tm, w), BF16),
        ],
        compiler_params=pltpu.CompilerParams(
            dimension_semantics=("arbitrary",), vmem_limit_bytes=VMEM_LIMIT),
        name="conv_out_proj_ffn",
    )(x2, attn_t, u, u, dw, db, lg, lb, pw, pb, og, gsum, woa, woc, gf, wup, wdn, gl)


def _group_sum_matrix(width):
    idx = jnp.arange(width) // HEAD_DIM
    return (idx[:, None] == idx[None, :]).astype(BF16)


def kernel(x, norm_mix_g, w_in, b_forget, conv_dw_w, conv_dw_b, conv_ln_g, conv_ln_b,
           w_conv_pw, b_conv_pw, attn_out_g, conv_out_g, w_out, norm_ffn_g,
           w_ffn_up, w_ffn_down, norm_final_g):
    b, s, d = x.shape
    depth = w_in.shape[0]
    heads = b_forget.shape[1]
    attn_w = heads * HEAD_DIM
    conv_w = conv_dw_w.shape[2]
    n = b * s
    x2 = x.reshape(n, d)
    gsum_conv = _group_sum_matrix(conv_w)
    for l in range(depth):
        w = w_in[l]
        wq, wk, wv = w[:, :attn_w], w[:, attn_w:2 * attn_w], w[:, 2 * attn_w:3 * attn_w]
        wqv_t = jnp.concatenate([wq, wv], axis=1).T.astype(BF16)
        wf = jnp.pad(w[:, 3 * attn_w + 2 * conv_w:], ((0, 0), (0, LANES - heads)))
        wrest = jnp.concatenate(
            [wk, w[:, 3 * attn_w:3 * attn_w + 2 * conv_w], wf], axis=1).astype(BF16)
        bfg = jnp.pad(b_forget[l], (0, LANES - heads)).reshape(1, LANES)
        q_t, k, v_t, u, lf = _in_proj(x2, norm_mix_g[l].reshape(1, d), wqv_t, wrest, bfg,
                                      attn_w=attn_w, conv_w=conv_w, tm=1024)

        kaug, c_row = _forget_cumsum(lf.reshape(b, s, LANES), heads, blk=256)
        attn_t = _attention(q_t, k.reshape(b, s, attn_w), kaug, v_t, c_row,
                            attn_out_g[l].reshape(attn_w, 1), tq=256)

        vec = lambda a: a.reshape(1, conv_w)
        wo = w_out[l].astype(BF16)
        x2 = _post(x2, attn_t, u, conv_dw_w[l], vec(conv_dw_b[l]), vec(conv_ln_g[l]),
                   vec(conv_ln_b[l]), w_conv_pw[l].astype(BF16), vec(b_conv_pw[l]),
                   vec(conv_out_g[l]), gsum_conv, wo[:attn_w], wo[attn_w:],
                   norm_ffn_g[l].reshape(1, d), w_ffn_up[l].astype(BF16),
                   w_ffn_down[l].astype(BF16), norm_final_g.reshape(1, d),
                   tm=512, seq=s, apply_final_norm=(l == depth - 1))
    return x2.reshape(b, s, d)
```

```python
import functools

import jax
import jax.numpy as jnp
from jax import lax
from jax.experimental import pallas as pl
from jax.experimental.pallas import tpu as pltpu

F32 = jnp.float32
BF16 = jnp.bfloat16

HEAD_DIM = 64
CONV_KERNEL = 31
EPS = 1e-6
LANES = 128
SUBLANES = 8
VMEM_LIMIT = 56 * 1024 * 1024
NEG_BIG = -1e30
LOG2E = 1.4426950408889634
N_SPLIT = 3
NT_DIMS = (((1,), (1,)), ((), ()))
TN_DIMS = (((0,), (0,)), ((), ()))


def _rms(x, g):
    ms = jnp.mean(x * x, axis=-1, keepdims=True)
    return x * lax.rsqrt(ms + EPS) * g


def _split3(x):
    hi = x.astype(BF16)
    r = x - hi.astype(F32)
    mid = r.astype(BF16)
    lo = (r - mid.astype(F32)).astype(BF16)
    return hi, mid, lo


def _in_proj_kernel(x_ref, g_ref, wqvf_t_ref, wrest_ref, bf_ref,
                    qt_ref, k_ref, vt_ref, u_ref, lf_ref, *, attn_w, conv_w, parts):
    tm = x_ref.shape[0]
    heads = lf_ref.shape[0]
    halves = [slice(i * (tm // parts), (i + 1) * (tm // parts)) for i in range(parts)]
    hs = [_rms(x_ref[r, :], g_ref[...]).astype(BF16) for r in halves]
    for r, h in zip(halves, hs):
        qvf_t = lax.dot_general(wqvf_t_ref[...], h, NT_DIMS, preferred_element_type=F32)
        qt_ref[:, r] = (qvf_t[:attn_w] * (LOG2E * HEAD_DIM ** -0.5)).astype(BF16)
        vt_ref[:, r] = qvf_t[attn_w:2 * attn_w].astype(BF16)
        fl = qvf_t[2 * attn_w:2 * attn_w + heads] + bf_ref[...]
        lf_ref[:, r] = jnp.minimum(fl, 0.0) - jnp.log1p(jnp.exp(-jnp.abs(fl)))
    for r, h in zip(halves, hs):
        rest = jnp.dot(h, wrest_ref[...], preferred_element_type=F32)
        k_ref[r, :] = rest[:, :attn_w].astype(BF16)
        ga = rest[:, attn_w:attn_w + conv_w]
        gb = rest[:, attn_w + conv_w:]
        u_ref[r, :] = ga * (1.0 / (1.0 + jnp.exp(-gb)))


def _in_proj(x2, g, wqv_t, wrest, bf, *, attn_w, conv_w, tm):
    n, d = x2.shape
    heads = bf.shape[0]
    const = lambda i: (0, 0)
    row = lambda i: (i, 0)
    col = lambda i: (0, i)
    return pl.pallas_call(
        functools.partial(_in_proj_kernel, attn_w=attn_w, conv_w=conv_w, parts=4),
        grid=(n // tm,),
        in_specs=[
            pl.BlockSpec((tm, d), row),
            pl.BlockSpec((1, d), const),
            pl.BlockSpec(wqv_t.shape, const),
            pl.BlockSpec(wrest.shape, const),
            pl.BlockSpec(bf.shape, const),
        ],
        out_specs=[
            pl.BlockSpec((attn_w, tm), col),
            pl.BlockSpec((tm, attn_w), row),
            pl.BlockSpec((attn_w, tm), col),
            pl.BlockSpec((tm, conv_w), row),
            pl.BlockSpec((heads, tm), col),
        ],
        out_shape=[
            jax.ShapeDtypeStruct((attn_w, n), BF16),
            jax.ShapeDtypeStruct((n, attn_w), BF16),
            jax.ShapeDtypeStruct((attn_w, n), BF16),
            jax.ShapeDtypeStruct((n, conv_w), F32),
            jax.ShapeDtypeStruct((heads, n), F32),
        ],
        compiler_params=pltpu.CompilerParams(
            dimension_semantics=("arbitrary",), vmem_limit_bytes=VMEM_LIMIT),
        name="in_proj",
    )(x2, g, wqv_t, wrest, bf)


def _cumsum_kernel(lf_ref, tri_ref, scat_ref, ones_ref, kaug_ref, crow_ref, *, blk, heads):
    assert heads == SUBLANES
    seq = lf_ref.shape[1]
    nblk = seq // blk
    tri_u = tri_ref[...]

    def stack3(x):
        terms = [t.astype(F32) for t in _split3(x)]
        return jnp.concatenate(terms + [jnp.zeros_like(x)], axis=0).astype(BF16)

    local = []
    for r in range(nblk):
        sums = jnp.dot(stack3(lf_ref[:, r * blk:(r + 1) * blk]), tri_u,
                       preferred_element_type=F32)
        local.append(sum(sums[t * heads:(t + 1) * heads, :] for t in range(N_SPLIT)))
    offset = jnp.zeros((heads, 1), F32)
    for r in range(nblk):
        c = local[r] + offset
        offset = c[:, blk - 1:blk]
        c = c * LOG2E
        crow_ref[0, :, r * blk:(r + 1) * blk] = c
        aug = ones_ref[...] + lax.dot_general(stack3(-c), scat_ref[...], TN_DIMS,
                                              preferred_element_type=F32)
        kaug_ref[0, r * blk:(r + 1) * blk, :] = aug.astype(BF16)


def _forget_cumsum(lf_t, b, *, blk):
    heads, n = lf_t.shape
    s = n // b
    tri = (jnp.arange(blk)[:, None] <= jnp.arange(blk)[None, :]).astype(BF16)
    lane = jnp.arange(LANES)
    row = jnp.arange(4 * heads)
    src_t, src_h = row[:, None] // heads, row[:, None] % heads
    scat = ((lane[None, :] == N_SPLIT + N_SPLIT * src_h + src_t) & (src_t < N_SPLIT)).astype(BF16)
    ones_row = (lane < N_SPLIT).astype(F32).reshape(1, LANES)
    const2 = lambda i: (0, 0)
    return pl.pallas_call(
        functools.partial(_cumsum_kernel, blk=blk, heads=heads),
        grid=(b,),
        in_specs=[
            pl.BlockSpec((heads, s), lambda i: (0, i)),
            pl.BlockSpec(tri.shape, const2),
            pl.BlockSpec(scat.shape, const2),
            pl.BlockSpec(ones_row.shape, const2),
        ],
        out_specs=[
            pl.BlockSpec((1, s, LANES), lambda i: (i, 0, 0)),
            pl.BlockSpec((1, heads, s), lambda i: (i, 0, 0)),
        ],
        out_shape=[
            jax.ShapeDtypeStruct((b, s, LANES), BF16),
            jax.ShapeDtypeStruct((b, heads, s), F32),
        ],
        compiler_params=pltpu.CompilerParams(dimension_semantics=("arbitrary",)),
        name="forget_cumsum",
    )(lf_t, tri, scat, ones_row)


def _allmax_sublanes(x):
    for shift in (4, 2, 1):
        x = jnp.maximum(x, pltpu.roll(x, shift, axis=0))
    return x


def _attn_kernel(qt_ref, k_ref, kaug_ref, vt_ref, crow_ref, g_ref, o_ref,
                 qrhs_ref, m_ref, l_ref, acc_ref, *, tq, heads):
    tk = tq
    n_q = k_ref.shape[1] // tq
    rowid = lax.broadcasted_iota(jnp.int32, (LANES, tq), 0)
    key_pos = lax.broadcasted_iota(jnp.int32, (tk, tq), 0)
    qry_pos = lax.broadcasted_iota(jnp.int32, (tk, tq), 1)
    causal = key_pos <= qry_pos

    def prepare(i, slot):
        qcols = slice(i * tq, (i + 1) * tq)
        for h in range(heads):
            pair, e = divmod(h, 2)
            q_pair = qt_ref[pair * LANES:(pair + 1) * LANES, qcols]
            own = (rowid >= HEAD_DIM * e) & (rowid < HEAD_DIM * (e + 1))
            qrhs_ref[slot, h, 0:LANES, :] = jnp.where(own, q_pair, jnp.zeros_like(q_pair))
            c_hi, c_mid, c_lo = (t.astype(F32) for t in _split3(crow_ref[0, h:h + 1, qcols]))
            own_lanes = (rowid >= N_SPLIT * (h + 1)) & (rowid < N_SPLIT * (h + 2))
            aug = jnp.where(rowid == 0, c_hi,
                            jnp.where(rowid == 1, c_mid,
                                      jnp.where(rowid == 2, c_lo,
                                                jnp.where(own_lanes, 1.0, 0.0))))
            qrhs_ref[slot, h, LANES:2 * LANES, :] = aug.astype(BF16)
        m_ref[slot] = jnp.full(m_ref.shape[1:], NEG_BIG, F32)
        l_ref[slot] = jnp.zeros(l_ref.shape[1:], F32)
        acc_ref[slot] = jnp.zeros(acc_ref.shape[1:], F32)

    ones_rows = jnp.ones((2 * SUBLANES, tk), BF16)

    def scores(i, j, h):
        keys = slice(j * tk, (j + 1) * tk)
        pair = h // 2
        k_lhs = jnp.concatenate(
            [k_ref[0, keys, pair * LANES:(pair + 1) * LANES], kaug_ref[0, keys, :]], axis=1)
        return jnp.dot(k_lhs, qrhs_ref[i % 2, h], preferred_element_type=F32)

    def absorb(i, j, h, s_t):
        slot = i % 2
        keys = slice(j * tk, (j + 1) * tk)
        if j == i:
            s_t = jnp.where(causal, s_t, NEG_BIG)
        s3 = s_t.reshape(tk // SUBLANES, SUBLANES, tq)
        m_old = m_ref[slot, h]
        m_new = jnp.maximum(m_old, _allmax_sublanes(jnp.max(s3, axis=0)))
        alpha = jnp.exp2(m_old - m_new)
        p = jnp.exp2(s3 - m_new[None]).reshape(tk, tq).astype(BF16)
        rows = slice(h * HEAD_DIM, (h + 1) * HEAD_DIM)
        v_aug = jnp.concatenate([vt_ref[rows, keys], ones_rows], axis=0)
        pv = jnp.dot(v_aug, p, preferred_element_type=F32)
        l_ref[slot, h] = alpha * l_ref[slot, h] + pv[HEAD_DIM:HEAD_DIM + SUBLANES, :]
        acc3 = acc_ref[slot, rows, :].reshape(HEAD_DIM // SUBLANES, SUBLANES, tq)
        pv3 = pv[:HEAD_DIM, :].reshape(HEAD_DIM // SUBLANES, SUBLANES, tq)
        acc_ref[slot, rows, :] = (alpha[None] * acc3 + pv3).reshape(HEAD_DIM, tq)
        m_ref[slot, h] = m_new

    def finish(i, slot):
        qcols = slice(i * tq, (i + 1) * tq)
        for h in range(heads):
            rows = slice(h * HEAD_DIM, (h + 1) * HEAD_DIM)
            acc3 = acc_ref[slot, rows, :].reshape(HEAD_DIM // SUBLANES, SUBLANES, tq)
            o = (acc3 / l_ref[slot, h][None]).reshape(HEAD_DIM, tq)
            ms = jnp.mean(o * o, axis=0, keepdims=True)
            o_ref[rows, qcols] = (o * lax.rsqrt(ms + EPS) * g_ref[rows, :]).astype(o_ref.dtype)

    chains = [(i, j, h) for i in range(n_q) for j in range(i + 1) for h in range(heads)]
    lookahead = 12
    assert lookahead <= 2 * heads
    prepared = set()
    pending = {}

    def issue(p):
        if p < len(chains):
            i = chains[p][0]
            if i not in prepared:
                prepare(i, i % 2)
                prepared.add(i)
            pending[p] = scores(*chains[p])

    for p in range(lookahead):
        issue(p)
    for p, (i, j, h) in enumerate(chains):
        absorb(i, j, h, pending.pop(p))
        if j == i and h == heads - 1:
            finish(i, i % 2)
        issue(p + lookahead)


def _attention(q_t, k3, kaug, v_t, c_row, g_col, *, tq):
    b, s, w = k3.shape
    heads = w // HEAD_DIM
    return pl.pallas_call(
        functools.partial(_attn_kernel, tq=tq, heads=heads),
        grid=(b,),
        in_specs=[
            pl.BlockSpec((w, s), lambda bi: (0, bi)),
            pl.BlockSpec((1, s, w), lambda bi: (bi, 0, 0)),
            pl.BlockSpec((1, s, LANES), lambda bi: (bi, 0, 0)),
            pl.BlockSpec((w, s), lambda bi: (0, bi)),
            pl.BlockSpec((1, heads, s), lambda bi: (bi, 0, 0)),
            pl.BlockSpec((w, 1), lambda bi: (0, 0)),
        ],
        out_specs=pl.BlockSpec((w, s), lambda bi: (0, bi)),
        out_shape=jax.ShapeDtypeStruct((w, b * s), BF16),
        scratch_shapes=[
            pltpu.VMEM((2, heads, 2 * LANES, tq), BF16),
            pltpu.VMEM((2, heads, SUBLANES, tq), F32),
            pltpu.VMEM((2, heads, SUBLANES, tq), F32),
            pltpu.VMEM((2, w, tq), F32),
        ],
        compiler_params=pltpu.CompilerParams(
            dimension_semantics=("arbitrary",), vmem_limit_bytes=VMEM_LIMIT),
        name="fox_attention",
    )(q_t, k3, kaug, v_t, c_row, g_col)


def _depthwise_strips(u_ref, halo_ref, dw_ref, db_ref, pad_ref, dwout_ref, *, first_in_seq, sub_rows):
    tm, width = u_ref.shape
    halo = halo_ref.shape[0]
    prev = halo_ref[...]
    pad_ref[0:halo, :] = jnp.where(first_in_seq, jnp.zeros_like(prev), prev)
    pad_ref[halo:, :] = u_ref[...]
    lead = halo - (CONV_KERNEL - 1)
    window = sub_rows + halo

    def strip(sb, cb):
        lanes = slice(cb * LANES, (cb + 1) * LANES)
        x_win = pad_ref[sb * sub_rows:sb * sub_rows + window, lanes]
        acc = jnp.zeros((sub_rows, LANES), F32) + db_ref[:, lanes]
        for res in range(SUBLANES):
            taps = [t for t in range(CONV_KERNEL) if (lead + t) % SUBLANES == res]
            shifted = x_win if res == 0 else pltpu.roll(x_win, window - res, axis=0)
            for t in taps:
                a0 = (lead + t) - res
                acc = acc + shifted[a0:a0 + sub_rows, :] * dw_ref[t:t + 1, lanes]
        dwout_ref[sb * sub_rows:(sb + 1) * sub_rows, lanes] = acc
        return jnp.sum(acc, axis=0, keepdims=True)

    return [functools.partial(strip, sb, cb)
            for sb in range(tm // sub_rows) for cb in range(width // LANES)]


def _norm_swish(x, lg_ref, lb_ref):
    mu = jnp.mean(x, axis=-1, keepdims=True)
    xc = x - mu
    var = jnp.mean(xc * xc, axis=-1, keepdims=True)
    y = xc * lax.rsqrt(var + EPS) * lg_ref[...] + lb_ref[...]
    return y * (1.0 / (1.0 + jnp.exp(-y)))


def _post_kernel(x_ref, at_ref, u_ref, halo_ref, dw_ref, db_ref, lg_ref, lb_ref, pw_ref, pb_ref,
                 og_ref, gsum_ref, woa_ref, woc_ref, gf_ref, wup_ref, wdn_ref, gl_ref,
                 o_ref, pad_ref, dwout_ref, y_ref,
                 *, tiles_per_seq, n_tiles, sub_rows, ff_chunks, apply_final_norm):
    g = pl.program_id(0)

    @pl.when(g == 0)
    def _():
        y_ref[...] = jnp.zeros(y_ref.shape, y_ref.dtype)

    tile = jnp.minimum(g, n_tiles - 1)
    first_in_seq = lax.rem(tile, tiles_per_seq) == 0
    strips = _depthwise_strips(u_ref, halo_ref, dw_ref, db_ref, pad_ref, dwout_ref,
                               first_in_seq=first_in_seq, sub_rows=sub_rows)
    tm = o_ref.shape[0]
    ff = wup_ref.shape[1] // ff_chunks
    bounds = [round(w * len(strips) / (ff_chunks - 1)) for w in range(ff_chunks)]
    norm_rows = tm // 4
    top = 2 * SUBLANES

    def norm_block(r):
        rows = slice(r * norm_rows, (r + 1) * norm_rows)
        y = _norm_swish(dwout_ref[rows, :], lg_ref, lb_ref)
        y_ref[rows, :] = y.astype(y_ref.dtype)
        t = jnp.sum(y, axis=0, keepdims=True)
        return sum(t[:, i * LANES:(i + 1) * LANES] for i in range(y.shape[1] // LANES))

    def tied(lhs, work):
        tokens = [item() for item in work]
        t = sum(tokens[1:], tokens[0])
        corner = jnp.where(g >= 0, lhs[:top, :LANES], jnp.broadcast_to(t, (top, LANES)).astype(BF16))
        return jnp.concatenate(
            [jnp.concatenate([corner, lhs[:top, LANES:]], axis=1), lhs[top:, :]], axis=0)

    halves = [slice(i * (tm // 2), (i + 1) * (tm // 2)) for i in range(2)]
    z = [jnp.dot(y_ref[r, :], pw_ref[...], preferred_element_type=F32) + pb_ref[...] for r in halves]
    x_attn = [x_ref[r, :] + lax.dot_general(at_ref[:, r], woa_ref[...], TN_DIMS,
                                            preferred_element_type=F32) for r in halves]
    h = []
    for i, r in enumerate(halves):
        ms = jnp.dot((z[i] * z[i]).astype(BF16), gsum_ref[...],
                     preferred_element_type=F32) * (1.0 / HEAD_DIM)
        conv = (z[i] * lax.rsqrt(ms + EPS) * og_ref[...]).astype(BF16)
        x1 = x_attn[i] + jnp.dot(conv, woc_ref[...], preferred_element_type=F32)
        o_ref[r, :] = x1
        h.append(_rms(x1, gf_ref[...]).astype(BF16))
    for c in range(ff_chunks):
        cols = slice(c * ff, (c + 1) * ff)
        lhs = [tied(h[0], strips[bounds[c - 1]:bounds[c]]) if c else h[0], h[1]]
        a2 = []
        for h_i in lhs:
            a = jnp.maximum(jnp.dot(h_i, wup_ref[:, cols], preferred_element_type=F32), 0.0)
            a2.append((a * a).astype(BF16))
        if c == ff_chunks - 1:
            a2[1] = tied(a2[1], [functools.partial(norm_block, r) for r in range(tm // norm_rows)])
        for r, a2_i in zip(halves, a2):
            o_ref[r, :] += jnp.dot(a2_i, wdn_ref[cols, :], preferred_element_type=F32)
    if apply_final_norm:
        for r in halves:
            o_ref[r, :] = _rms(o_ref[r, :], gl_ref[...])


def _post(x2, attn_t, u, dw, db, lg, lb, pw, pb, og, gsum, woa, woc, gf, wup, wdn, gl,
          *, tm, seq, apply_final_norm):
    n, d = x2.shape
    w = u.shape[1]
    halo = 32
    n_tiles = n // tm
    const = lambda g: (0, 0)
    prev_tile = lambda g: (jnp.maximum(g - 1, 0), 0)
    cur_tile = lambda g: (jnp.minimum(g, n_tiles - 1), 0)
    halo_blk = lambda g: (jnp.maximum(jnp.minimum(g, n_tiles - 1) * (tm // halo) - 1, 0), 0)
    vec = pl.BlockSpec((1, w), const)
    resident = lambda a: pl.BlockSpec(a.shape, const, pipeline_mode=pl.Buffered(1))
    return pl.pallas_call(
        functools.partial(_post_kernel, tiles_per_seq=seq // tm, n_tiles=n_tiles, sub_rows=tm // 8,
                          ff_chunks=4,
                          apply_final_norm=apply_final_norm),
        grid=(n_tiles + 1,),
        in_specs=[
            pl.BlockSpec((tm, d), prev_tile),
            pl.BlockSpec((attn_t.shape[0], tm), lambda g: (0, jnp.maximum(g - 1, 0))),
            pl.BlockSpec((tm, w), cur_tile),
            pl.BlockSpec((halo, w), halo_blk),
            pl.BlockSpec(dw.shape, const),
            vec, vec, vec,
            pl.BlockSpec(pw.shape, const),
            vec, vec,
            pl.BlockSpec(gsum.shape, const),
            resident(woa),
            resident(woc),
            pl.BlockSpec((1, d), const),
            resident(wup),
            resident(wdn),
            pl.BlockSpec((1, d), const),
        ],
        out_specs=pl.BlockSpec((tm, d), prev_tile),
        out_shape=jax.ShapeDtypeStruct((n, d), F32),
        scratch_shapes=[
            pltpu.VMEM((tm + halo, w), F32),
            pltpu.VMEM((tm, w), F32),
            pltpu.VMEM((tm, w), BF16),
        ],
        compiler_params=pltpu.CompilerParams(
            dimension_semantics=("arbitrary",), vmem_limit_bytes=VMEM_LIMIT),
        name="conv_out_proj_ffn",
    )(x2, attn_t, u, u, dw, db, lg, lb, pw, pb, og, gsum, woa, woc, gf, wup, wdn, gl)


def _group_sum_matrix(width):
    idx = jnp.arange(width) // HEAD_DIM
    return (idx[:, None] == idx[None, :]).astype(BF16)


def kernel(x, norm_mix_g, w_in, b_forget, conv_dw_w, conv_dw_b, conv_ln_g, conv_ln_b,
           w_conv_pw, b_conv_pw, attn_out_g, conv_out_g, w_out, norm_ffn_g,
           w_ffn_up, w_ffn_down, norm_final_g):
    b, s, d = x.shape
    depth = w_in.shape[0]
    heads = b_forget.shape[1]
    attn_w = heads * HEAD_DIM
    conv_w = conv_dw_w.shape[2]
    n = b * s
    x2 = x.reshape(n, d)
    gsum_conv = _group_sum_matrix(conv_w)
    for l in range(depth):
        w = w_in[l]
        wq, wk, wv = w[:, :attn_w], w[:, attn_w:2 * attn_w], w[:, 2 * attn_w:3 * attn_w]
        wf = jnp.pad(w[:, 3 * attn_w + 2 * conv_w:], ((0, 0), (0, 2 * SUBLANES - heads)))
        wqvf_t = jnp.concatenate([wq, wv, wf], axis=1).T.astype(BF16)
        wrest = jnp.concatenate([wk, w[:, 3 * attn_w:3 * attn_w + 2 * conv_w]], axis=1).astype(BF16)
        q_t, k, v_t, u, lf_t = _in_proj(x2, norm_mix_g[l].reshape(1, d), wqvf_t, wrest,
                                        b_forget[l].reshape(heads, 1),
                                        attn_w=attn_w, conv_w=conv_w, tm=1024)

        kaug, c_row = _forget_cumsum(lf_t, b, blk=256)
        attn_t = _attention(q_t, k.reshape(b, s, attn_w), kaug, v_t, c_row,
                            attn_out_g[l].reshape(attn_w, 1), tq=256)

        vec = lambda a: a.reshape(1, conv_w)
        wo = w_out[l].astype(BF16)
        x2 = _post(x2, attn_t, u, conv_dw_w[l], vec(conv_dw_b[l]), vec(conv_ln_g[l]),
                   vec(conv_ln_b[l]), w_conv_pw[l].astype(BF16), vec(b_conv_pw[l]),
                   vec(conv_out_g[l]), gsum_conv, wo[:attn_w], wo[attn_w:],
                   norm_ffn_g[l].reshape(1, d), w_ffn_up[l].astype(BF16),
                   w_ffn_down[l].astype(BF16), norm_final_g.reshape(1, d),
                   tm=512, seq=s, apply_final_norm=(l == depth - 1))
    return x2.reshape(b, s, d)
```

```python
import functools

import jax
import jax.numpy as jnp
from jax import lax
from jax.experimental import pallas as pl
from jax.experimental.pallas import tpu as pltpu

F32 = jnp.float32
BF16 = jnp.bfloat16

HEAD_DIM = 64
CONV_KERNEL = 31
EPS = 1e-6
LANES = 128
SUBLANES = 8
VMEM_LIMIT = 56 * 1024 * 1024
NEG_BIG = -1e30
LOG2E = 1.4426950408889634
N_SPLIT = 3
NT_DIMS = (((1,), (1,)), ((), ()))
TN_DIMS = (((0,), (0,)), ((), ()))


def _rms(x, g):
    ms = jnp.mean(x * x, axis=-1, keepdims=True)
    return x * lax.rsqrt(ms + EPS) * g


def _split3(x):
    hi = x.astype(BF16)
    r = x - hi.astype(F32)
    mid = r.astype(BF16)
    lo = (r - mid.astype(F32)).astype(BF16)
    return hi, mid, lo


def _in_proj_kernel(x_ref, g_ref, wqvf_t_ref, wrest_ref, bf_ref, *refs, attn_w, conv_w, parts, n_cast):
    cast_in, (qt_ref, k_ref, vt_ref, u_ref, lf_ref), cast_out = (
        refs[:n_cast], refs[n_cast:n_cast + 5], refs[n_cast + 5:])
    for src, dst in zip(cast_in, cast_out):
        dst[...] = src[...].astype(dst.dtype)
    tm = x_ref.shape[0]
    heads = lf_ref.shape[0]
    halves = [slice(i * (tm // parts), (i + 1) * (tm // parts)) for i in range(parts)]
    hs = [_rms(x_ref[r, :], g_ref[...]).astype(BF16) for r in halves]
    for r, h in zip(halves, hs):
        qvf_t = lax.dot_general(wqvf_t_ref[...], h, NT_DIMS, preferred_element_type=F32)
        qt_ref[:, r] = (qvf_t[:attn_w] * (LOG2E * HEAD_DIM ** -0.5)).astype(BF16)
        vt_ref[:, r] = qvf_t[attn_w:2 * attn_w].astype(BF16)
        fl = qvf_t[2 * attn_w:2 * attn_w + heads] + bf_ref[...]
        lf_ref[:, r] = jnp.minimum(fl, 0.0) - jnp.log1p(jnp.exp(-jnp.abs(fl)))
    for r, h in zip(halves, hs):
        rest = jnp.dot(h, wrest_ref[...], preferred_element_type=F32)
        k_ref[r, :] = rest[:, :attn_w].astype(BF16)
        ga = rest[:, attn_w:attn_w + conv_w]
        gb = rest[:, attn_w + conv_w:]
        u_ref[r, :] = ga * (1.0 / (1.0 + jnp.exp(-gb)))


def _in_proj(x2, g, wqv_t, wrest, bf, to_cast, *, attn_w, conv_w, tm):
    n, d = x2.shape
    heads = bf.shape[0]
    steps = n // tm
    const = lambda i: (0, 0)
    row = lambda i: (i, 0)
    col = lambda i: (0, i)
    cast_specs = [pl.BlockSpec((a.shape[0] // steps, a.shape[1]), row) for a in to_cast]
    assert all(a.shape[0] % (steps * 2 * SUBLANES) == 0 for a in to_cast)
    outs = pl.pallas_call(
        functools.partial(_in_proj_kernel, attn_w=attn_w, conv_w=conv_w, parts=4,
                          n_cast=len(to_cast)),
        grid=(steps,),
        in_specs=[
            pl.BlockSpec((tm, d), row),
            pl.BlockSpec((1, d), const),
            pl.BlockSpec(wqv_t.shape, const),
            pl.BlockSpec(wrest.shape, const),
            pl.BlockSpec(bf.shape, const),
        ] + cast_specs,
        out_specs=[
            pl.BlockSpec((attn_w, tm), col),
            pl.BlockSpec((tm, attn_w), row),
            pl.BlockSpec((attn_w, tm), col),
            pl.BlockSpec((tm, conv_w), row),
            pl.BlockSpec((heads, tm), col),
        ] + cast_specs,
        out_shape=[
            jax.ShapeDtypeStruct((attn_w, n), BF16),
            jax.ShapeDtypeStruct((n, attn_w), BF16),
            jax.ShapeDtypeStruct((attn_w, n), BF16),
            jax.ShapeDtypeStruct((n, conv_w), F32),
            jax.ShapeDtypeStruct((heads, n), F32),
        ] + [jax.ShapeDtypeStruct(a.shape, BF16) for a in to_cast],
        compiler_params=pltpu.CompilerParams(
            dimension_semantics=("arbitrary",), vmem_limit_bytes=VMEM_LIMIT),
        name="in_proj",
    )(x2, g, wqv_t, wrest, bf, *to_cast)
    return outs[:5], outs[5:]


def _cumsum_kernel(lf_ref, tri_ref, scat_ref, ones_ref, kaug_ref, crow_ref, *, blk, heads):
    assert heads == SUBLANES
    seq = lf_ref.shape[1]
    nblk = seq // blk
    tri_u = tri_ref[...]

    def stack3(x):
        terms = [t.astype(F32) for t in _split3(x)]
        return jnp.concatenate(terms + [jnp.zeros_like(x)], axis=0).astype(BF16)

    local = []
    for r in range(nblk):
        sums = jnp.dot(stack3(lf_ref[:, r * blk:(r + 1) * blk]), tri_u,
                       preferred_element_type=F32)
        local.append(sum(sums[t * heads:(t + 1) * heads, :] for t in range(N_SPLIT)))
    offset = jnp.zeros((heads, 1), F32)
    for r in range(nblk):
        c = local[r] + offset
        offset = c[:, blk - 1:blk]
        c = c * LOG2E
        crow_ref[0, :, r * blk:(r + 1) * blk] = c
        aug = ones_ref[...] + lax.dot_general(stack3(-c), scat_ref[...], TN_DIMS,
                                              preferred_element_type=F32)
        kaug_ref[0, r * blk:(r + 1) * blk, :] = aug.astype(BF16)


def _forget_cumsum(lf_t, b, *, blk):
    heads, n = lf_t.shape
    s = n // b
    tri = (jnp.arange(blk)[:, None] <= jnp.arange(blk)[None, :]).astype(BF16)
    lane = jnp.arange(LANES)
    row = jnp.arange(4 * heads)
    src_t, src_h = row[:, None] // heads, row[:, None] % heads
    scat = ((lane[None, :] == N_SPLIT + N_SPLIT * src_h + src_t) & (src_t < N_SPLIT)).astype(BF16)
    ones_row = (lane < N_SPLIT).astype(F32).reshape(1, LANES)
    const2 = lambda i: (0, 0)
    return pl.pallas_call(
        functools.partial(_cumsum_kernel, blk=blk, heads=heads),
        grid=(b,),
        in_specs=[
            pl.BlockSpec((heads, s), lambda i: (0, i)),
            pl.BlockSpec(tri.shape, const2),
            pl.BlockSpec(scat.shape, const2),
            pl.BlockSpec(ones_row.shape, const2),
        ],
        out_specs=[
            pl.BlockSpec((1, s, LANES), lambda i: (i, 0, 0)),
            pl.BlockSpec((1, heads, s), lambda i: (i, 0, 0)),
        ],
        out_shape=[
            jax.ShapeDtypeStruct((b, s, LANES), BF16),
            jax.ShapeDtypeStruct((b, heads, s), F32),
        ],
        compiler_params=pltpu.CompilerParams(dimension_semantics=("arbitrary",)),
        name="forget_cumsum",
    )(lf_t, tri, scat, ones_row)


def _allmax_sublanes(x):
    for shift in (4, 2, 1):
        x = jnp.maximum(x, pltpu.roll(x, shift, axis=0))
    return x


def _attn_kernel(qt_ref, k_ref, kaug_ref, vt_ref, crow_ref, g_ref, o_ref,
                 qrhs_ref, m_ref, l_ref, acc_ref, *, tq, tk, heads):
    n_q = k_ref.shape[1] // tq
    kt = tq // tk
    rowid = lax.broadcasted_iota(jnp.int32, (LANES, tq), 0)
    key_pos = lax.broadcasted_iota(jnp.int32, (tk, tq), 0)
    qry_pos = lax.broadcasted_iota(jnp.int32, (tk, tq), 1)
    causal = [key_pos + d * tk <= qry_pos for d in range(kt)]

    def prepare(i, slot):
        qcols = slice(i * tq, (i + 1) * tq)
        for h in range(heads):
            pair, e = divmod(h, 2)
            q_pair = qt_ref[pair * LANES:(pair + 1) * LANES, qcols]
            own = (rowid >= HEAD_DIM * e) & (rowid < HEAD_DIM * (e + 1))
            qrhs_ref[slot, h, 0:LANES, :] = jnp.where(own, q_pair, jnp.zeros_like(q_pair))
            c_hi, c_mid, c_lo = (t.astype(F32) for t in _split3(crow_ref[0, h:h + 1, qcols]))
            own_lanes = (rowid >= N_SPLIT * (h + 1)) & (rowid < N_SPLIT * (h + 2))
            aug = jnp.where(rowid == 0, c_hi,
                            jnp.where(rowid == 1, c_mid,
                                      jnp.where(rowid == 2, c_lo,
                                                jnp.where(own_lanes, 1.0, 0.0))))
            qrhs_ref[slot, h, LANES:2 * LANES, :] = aug.astype(BF16)
        m_ref[slot] = jnp.full(m_ref.shape[1:], NEG_BIG, F32)
        l_ref[slot] = jnp.zeros(l_ref.shape[1:], F32)
        acc_ref[slot] = jnp.zeros(acc_ref.shape[1:], F32)

    ones_rows = jnp.ones((2 * SUBLANES, tk), BF16)

    def scores(i, j, h):
        keys = slice(j * tk, (j + 1) * tk)
        pair = h // 2
        k_lhs = jnp.concatenate(
            [k_ref[0, keys, pair * LANES:(pair + 1) * LANES], kaug_ref[0, keys, :]], axis=1)
        return jnp.dot(k_lhs, qrhs_ref[i % 2, h], preferred_element_type=F32)

    def absorb(i, j, h, s_t):
        slot = i % 2
        keys = slice(j * tk, (j + 1) * tk)
        if j >= i * kt:
            s_t = jnp.where(causal[j - i * kt], s_t, NEG_BIG)
        s3 = s_t.reshape(tk // SUBLANES, SUBLANES, tq)
        m_old = m_ref[slot, h]
        m_new = jnp.maximum(m_old, _allmax_sublanes(jnp.max(s3, axis=0)))
        alpha = jnp.exp2(m_old - m_new)
        p = jnp.exp2(s3 - m_new[None]).reshape(tk, tq).astype(BF16)
        rows = slice(h * HEAD_DIM, (h + 1) * HEAD_DIM)
        v_aug = jnp.concatenate([vt_ref[rows, keys], ones_rows], axis=0)
        pv = jnp.dot(v_aug, p, preferred_element_type=F32)
        l_ref[slot, h] = alpha * l_ref[slot, h] + pv[HEAD_DIM:HEAD_DIM + SUBLANES, :]
        acc3 = acc_ref[slot, rows, :].reshape(HEAD_DIM // SUBLANES, SUBLANES, tq)
        pv3 = pv[:HEAD_DIM, :].reshape(HEAD_DIM // SUBLANES, SUBLANES, tq)
        acc_ref[slot, rows, :] = (alpha[None] * acc3 + pv3).reshape(HEAD_DIM, tq)
        m_ref[slot, h] = m_new

    def finish(i, slot):
        qcols = slice(i * tq, (i + 1) * tq)
        for h in range(heads):
            rows = slice(h * HEAD_DIM, (h + 1) * HEAD_DIM)
            acc3 = acc_ref[slot, rows, :].reshape(HEAD_DIM // SUBLANES, SUBLANES, tq)
            o = (acc3 / l_ref[slot, h][None]).reshape(HEAD_DIM, tq)
            ms = jnp.mean(o * o, axis=0, keepdims=True)
            o_ref[rows, qcols] = (o * lax.rsqrt(ms + EPS) * g_ref[rows, :]).astype(o_ref.dtype)

    chains = [(i, j, h) for i in range(n_q) for j in range((i + 1) * kt) for h in range(heads)]
    lookahead = 12
    assert lookahead <= 2 * heads
    prepared = set()
    pending = {}

    def issue(p):
        if p < len(chains):
            i = chains[p][0]
            if i not in prepared:
                prepare(i, i % 2)
                prepared.add(i)
            pending[p] = scores(*chains[p])

    for p in range(lookahead):
        issue(p)
    for p, (i, j, h) in enumerate(chains):
        absorb(i, j, h, pending.pop(p))
        if j == (i + 1) * kt - 1 and h == heads - 1:
            finish(i, i % 2)
        issue(p + lookahead)


def _attention(q_t, k3, kaug, v_t, c_row, g_col, *, tq, tk):
    b, s, w = k3.shape
    heads = w // HEAD_DIM
    return pl.pallas_call(
        functools.partial(_attn_kernel, tq=tq, tk=tk, heads=heads),
        grid=(b,),
        in_specs=[
            pl.BlockSpec((w, s), lambda bi: (0, bi)),
            pl.BlockSpec((1, s, w), lambda bi: (bi, 0, 0)),
            pl.BlockSpec((1, s, LANES), lambda bi: (bi, 0, 0)),
            pl.BlockSpec((w, s), lambda bi: (0, bi)),
            pl.BlockSpec((1, heads, s), lambda bi: (bi, 0, 0)),
            pl.BlockSpec((w, 1), lambda bi: (0, 0)),
        ],
        out_specs=pl.BlockSpec((w, s), lambda bi: (0, bi)),
        out_shape=jax.ShapeDtypeStruct((w, b * s), BF16),
        scratch_shapes=[
            pltpu.VMEM((2, heads, 2 * LANES, tq), BF16),
            pltpu.VMEM((2, heads, SUBLANES, tq), F32),
            pltpu.VMEM((2, heads, SUBLANES, tq), F32),
            pltpu.VMEM((2, w, tq), F32),
        ],
        compiler_params=pltpu.CompilerParams(
            dimension_semantics=("arbitrary",), vmem_limit_bytes=VMEM_LIMIT),
        name="fox_attention",
    )(q_t, k3, kaug, v_t, c_row, g_col)


def _depthwise_strips(u_ref, halo_ref, dw_ref, db_ref, pad_ref, dwout_ref, *, first_in_seq, sub_rows):
    tm, width = u_ref.shape
    halo = halo_ref.shape[0]
    prev = halo_ref[...]
    pad_ref[0:halo, :] = jnp.where(first_in_seq, jnp.zeros_like(prev), prev)
    pad_ref[halo:, :] = u_ref[...]
    lead = halo - (CONV_KERNEL - 1)
    window = sub_rows + halo

    def strip(sb, cb):
        lanes = slice(cb * LANES, (cb + 1) * LANES)
        x_win = pad_ref[sb * sub_rows:sb * sub_rows + window, lanes]
        acc = jnp.zeros((sub_rows, LANES), F32) + db_ref[:, lanes]
        for res in range(SUBLANES):
            taps = [t for t in range(CONV_KERNEL) if (lead + t) % SUBLANES == res]
            shifted = x_win if res == 0 else pltpu.roll(x_win, window - res, axis=0)
            for t in taps:
                a0 = (lead + t) - res
                acc = acc + shifted[a0:a0 + sub_rows, :] * dw_ref[t:t + 1, lanes]
        dwout_ref[sb * sub_rows:(sb + 1) * sub_rows, lanes] = acc
        return jnp.sum(acc, axis=0, keepdims=True)

    return [functools.partial(strip, sb, cb)
            for sb in range(tm // sub_rows) for cb in range(width // LANES)]


def _norm_swish(x, lg_ref, lb_ref):
    mu = jnp.mean(x, axis=-1, keepdims=True)
    xc = x - mu
    var = jnp.mean(xc * xc, axis=-1, keepdims=True)
    y = xc * lax.rsqrt(var + EPS) * lg_ref[...] + lb_ref[...]
    return y * (1.0 / (1.0 + jnp.exp(-y)))


def _post_kernel(x_ref, at_ref, u_ref, halo_ref, dw_ref, db_ref, lg_ref, lb_ref, pw_ref, pb_ref,
                 og_ref, gsum_ref, woa_ref, woc_ref, gf_ref, wup_ref, wdn_ref, gl_ref,
                 o_ref, pad_ref, dwout_ref, y_ref,
                 *, tiles_per_seq, n_tiles, sub_rows, part_rows, ff_chunks, apply_final_norm):
    g = pl.program_id(0)

    @pl.when(g == 0)
    def _():
        y_ref[...] = jnp.zeros(y_ref.shape, y_ref.dtype)

    tile = jnp.minimum(g, n_tiles - 1)
    first_in_seq = lax.rem(tile, tiles_per_seq) == 0
    strips = _depthwise_strips(u_ref, halo_ref, dw_ref, db_ref, pad_ref, dwout_ref,
                               first_in_seq=first_in_seq, sub_rows=sub_rows)
    tm = o_ref.shape[0]
    ff = wup_ref.shape[1] // ff_chunks
    bounds = [round(w * len(strips) / (ff_chunks - 1)) for w in range(ff_chunks)]
    norm_rows = tm // 4
    top = 2 * SUBLANES

    def norm_block(r):
        rows = slice(r * norm_rows, (r + 1) * norm_rows)
        y = _norm_swish(dwout_ref[rows, :], lg_ref, lb_ref)
        y_ref[rows, :] = y.astype(y_ref.dtype)
        t = jnp.sum(y, axis=0, keepdims=True)
        return sum(t[:, i * LANES:(i + 1) * LANES] for i in range(y.shape[1] // LANES))

    def tied(lhs, work):
        tokens = [item() for item in work]
        t = sum(tokens[1:], tokens[0])
        corner = jnp.where(g >= 0, lhs[:top, :LANES], jnp.broadcast_to(t, (top, LANES)).astype(BF16))
        return jnp.concatenate(
            [jnp.concatenate([corner, lhs[:top, LANES:]], axis=1), lhs[top:, :]], axis=0)

    halves = [slice(i * part_rows, (i + 1) * part_rows) for i in range(tm // part_rows)]
    z = [jnp.dot(y_ref[r, :], pw_ref[...], preferred_element_type=F32) + pb_ref[...] for r in halves]
    x_attn = [x_ref[r, :] + lax.dot_general(at_ref[:, r], woa_ref[...], TN_DIMS,
                                            preferred_element_type=F32) for r in halves]
    h = []
    for i, r in enumerate(halves):
        ms = jnp.dot((z[i] * z[i]).astype(BF16), gsum_ref[...],
                     preferred_element_type=F32) * (1.0 / HEAD_DIM)
        conv = (z[i] * lax.rsqrt(ms + EPS) * og_ref[...]).astype(BF16)
        x1 = x_attn[i] + jnp.dot(conv, woc_ref[...], preferred_element_type=F32)
        o_ref[r, :] = x1
        h.append(_rms(x1, gf_ref[...]).astype(BF16))
    for c in range(ff_chunks):
        cols = slice(c * ff, (c + 1) * ff)
        lhs = [tied(h[0], strips[bounds[c - 1]:bounds[c]]) if c else h[0]] + h[1:]
        a2 = []
        for h_i in lhs:
            a = jnp.maximum(jnp.dot(h_i, wup_ref[:, cols], preferred_element_type=F32), 0.0)
            a2.append((a * a).astype(BF16))
        if c == ff_chunks - 1:
            a2[-1] = tied(a2[-1], [functools.partial(norm_block, r) for r in range(tm // norm_rows)])
        for r, a2_i in zip(halves, a2):
            o_ref[r, :] += jnp.dot(a2_i, wdn_ref[cols, :], preferred_element_type=F32)
    if apply_final_norm:
        for r in halves:
            o_ref[r, :] = _rms(o_ref[r, :], gl_ref[...])


def _post(x2, attn_t, u, dw, db, lg, lb, pw, pb, og, gsum, woa, woc, gf, wup, wdn, gl,
          *, tm, seq, apply_final_norm):
    n, d = x2.shape
    w = u.shape[1]
    halo = 32
    n_tiles = n // tm
    const = lambda g: (0, 0)
    prev_tile = lambda g: (jnp.maximum(g - 1, 0), 0)
    cur_tile = lambda g: (jnp.minimum(g, n_tiles - 1), 0)
    halo_blk = lambda g: (jnp.maximum(jnp.minimum(g, n_tiles - 1) * (tm // halo) - 1, 0), 0)
    vec = pl.BlockSpec((1, w), const)
    resident = lambda a: pl.BlockSpec(a.shape, const, pipeline_mode=pl.Buffered(1))
    return pl.pallas_call(
        functools.partial(_post_kernel, tiles_per_seq=seq // tm, n_tiles=n_tiles, sub_rows=64,
                          part_rows=256, ff_chunks=4,
                          apply_final_norm=apply_final_norm),
        grid=(n_tiles + 1,),
        in_specs=[
            pl.BlockSpec((tm, d), prev_tile),
            pl.BlockSpec((attn_t.shape[0], tm), lambda g: (0, jnp.maximum(g - 1, 0))),
            pl.BlockSpec((tm, w), cur_tile),
            pl.BlockSpec((halo, w), halo_blk),
            pl.BlockSpec(dw.shape, const),
            vec, vec, vec,
            pl.BlockSpec(pw.shape, const),
            vec, vec,
            pl.BlockSpec(gsum.shape, const),
            resident(woa),
            resident(woc),
            pl.BlockSpec((1, d), const),
            resident(wup),
            resident(wdn),
            pl.BlockSpec((1, d), const),
        ],
        out_specs=pl.BlockSpec((tm, d), prev_tile),
        out_shape=jax.ShapeDtypeStruct((n, d), F32),
        scratch_shapes=[
            pltpu.VMEM((tm + halo, w), F32),
            pltpu.VMEM((tm, w), F32),
            pltpu.VMEM((tm, w), BF16),
        ],
        compiler_params=pltpu.CompilerParams(
            dimension_semantics=("arbitrary",), vmem_limit_bytes=VMEM_LIMIT),
        name="conv_out_proj_ffn",
    )(x2, attn_t, u, u, dw, db, lg, lb, pw, pb, og, gsum, woa, woc, gf, wup, wdn, gl)


def _group_sum_matrix(width):
    idx = jnp.arange(width) // HEAD_DIM
    return (idx[:, None] == idx[None, :]).astype(BF16)


def kernel(x, norm_mix_g, w_in, b_forget, conv_dw_w, conv_dw_b, conv_ln_g, conv_ln_b,
           w_conv_pw, b_conv_pw, attn_out_g, conv_out_g, w_out, norm_ffn_g,
           w_ffn_up, w_ffn_down, norm_final_g):
    b, s, d = x.shape
    depth = w_in.shape[0]
    heads = b_forget.shape[1]
    attn_w = heads * HEAD_DIM
    conv_w = conv_dw_w.shape[2]
    n = b * s
    x2 = x.reshape(n, d)
    gsum_conv = _group_sum_matrix(conv_w)
    for l in range(depth):
        w = w_in[l]
        wq, wk, wv = w[:, :attn_w], w[:, attn_w:2 * attn_w], w[:, 2 * attn_w:3 * attn_w]
        wf = jnp.pad(w[:, 3 * attn_w + 2 * conv_w:], ((0, 0), (0, 2 * SUBLANES - heads)))
        wqvf_t = jnp.concatenate([wq, wv, wf], axis=1).T.astype(BF16)
        wrest = jnp.concatenate([wk, w[:, 3 * attn_w:3 * attn_w + 2 * conv_w]], axis=1).astype(BF16)
        (q_t, k, v_t, u, lf_t), (wo, wup, wdn, wpw) = _in_proj(
            x2, norm_mix_g[l].reshape(1, d), wqvf_t, wrest, b_forget[l].reshape(heads, 1),
            [w_out[l], w_ffn_up[l], w_ffn_down[l], w_conv_pw[l]],
            attn_w=attn_w, conv_w=conv_w, tm=1024)

        kaug, c_row = _forget_cumsum(lf_t, b, blk=256)
        attn_t = _attention(q_t, k.reshape(b, s, attn_w), kaug, v_t, c_row,
                            attn_out_g[l].reshape(attn_w, 1), tq=256, tk=256)

        vec = lambda a: a.reshape(1, conv_w)
        x2 = _post(x2, attn_t, u, conv_dw_w[l], vec(conv_dw_b[l]), vec(conv_ln_g[l]),
                   vec(conv_ln_b[l]), wpw, vec(b_conv_pw[l]),
                   vec(conv_out_g[l]), gsum_conv, wo[:attn_w], wo[attn_w:],
                   norm_ffn_g[l].reshape(1, d), wup, wdn, norm_final_g.reshape(1, d),
                   tm=512, seq=s, apply_final_norm=(l == depth - 1))
    return x2.reshape(b, s, d)
```

```python
import functools

import jax
import jax.numpy as jnp
from jax import lax
from jax.experimental import pallas as pl
from jax.experimental.pallas import tpu as pltpu

F32 = jnp.float32
BF16 = jnp.bfloat16

HEAD_DIM = 64
CONV_KERNEL = 31
EPS = 1e-6
LANES = 128
SUBLANES = 8
VMEM_LIMIT = 56 * 1024 * 1024
NEG_BIG = -1e30
LOG2E = 1.4426950408889634
N_SPLIT = 3
NT_DIMS = (((1,), (1,)), ((), ()))
TN_DIMS = (((0,), (0,)), ((), ()))

IN_PROJ_ROWS = 1024
IN_PROJ_PARTS = 4
CUMSUM_BLOCK = 256
ATTN_TILE = 256
POST_ROWS = 512
POST_PART_ROWS = 256
POST_FF_CHUNKS = 4
CONV_STRIP_ROWS = 64


def _rms(x, g):
    ms = jnp.mean(x * x, axis=-1, keepdims=True)
    return x * lax.rsqrt(ms + EPS) * g


def _split3(x):
    hi = x.astype(BF16)
    r = x - hi.astype(F32)
    mid = r.astype(BF16)
    lo = (r - mid.astype(F32)).astype(BF16)
    return hi, mid, lo


def _in_proj_kernel(x_ref, g_ref, wqvf_t_ref, wrest_ref, bf_ref, *refs, attn_w, conv_w, parts, n_cast):
    cast_in, (qt_ref, k_ref, vt_ref, u_ref, lf_ref), cast_out = (
        refs[:n_cast], refs[n_cast:n_cast + 5], refs[n_cast + 5:])
    for src, dst in zip(cast_in, cast_out):
        dst[...] = src[...].astype(dst.dtype)
    tm = x_ref.shape[0]
    heads = lf_ref.shape[0]
    halves = [slice(i * (tm // parts), (i + 1) * (tm // parts)) for i in range(parts)]
    hs = [_rms(x_ref[r, :], g_ref[...]).astype(BF16) for r in halves]
    for r, h in zip(halves, hs):
        qvf_t = lax.dot_general(wqvf_t_ref[...], h, NT_DIMS, preferred_element_type=F32)
        qt_ref[:, r] = (qvf_t[:attn_w] * (LOG2E * HEAD_DIM ** -0.5)).astype(BF16)
        vt_ref[:, r] = qvf_t[attn_w:2 * attn_w].astype(BF16)
        fl = qvf_t[2 * attn_w:2 * attn_w + heads] + bf_ref[...]
        lf_ref[:, r] = jnp.minimum(fl, 0.0) - jnp.log1p(jnp.exp(-jnp.abs(fl)))
    for r, h in zip(halves, hs):
        rest = jnp.dot(h, wrest_ref[...], preferred_element_type=F32)
        k_ref[r, :] = rest[:, :attn_w].astype(BF16)
        ga = rest[:, attn_w:attn_w + conv_w]
        gb = rest[:, attn_w + conv_w:]
        u_ref[r, :] = ga * (1.0 / (1.0 + jnp.exp(-gb)))


def _in_proj(x2, g, wqv_t, wrest, bf, to_cast, *, attn_w, conv_w, tm):
    n, d = x2.shape
    heads = bf.shape[0]
    steps = n // tm
    const = lambda i: (0, 0)
    row = lambda i: (i, 0)
    col = lambda i: (0, i)
    cast_specs = [pl.BlockSpec((a.shape[0] // steps, a.shape[1]), row) for a in to_cast]
    assert all(a.shape[0] % (steps * 2 * SUBLANES) == 0 for a in to_cast)
    outs = pl.pallas_call(
        functools.partial(_in_proj_kernel, attn_w=attn_w, conv_w=conv_w, parts=IN_PROJ_PARTS,
                          n_cast=len(to_cast)),
        grid=(steps,),
        in_specs=[
            pl.BlockSpec((tm, d), row),
            pl.BlockSpec((1, d), const),
            pl.BlockSpec(wqv_t.shape, const),
            pl.BlockSpec(wrest.shape, const),
            pl.BlockSpec(bf.shape, const),
        ] + cast_specs,
        out_specs=[
            pl.BlockSpec((attn_w, tm), col),
            pl.BlockSpec((tm, attn_w), row),
            pl.BlockSpec((attn_w, tm), col),
            pl.BlockSpec((tm, conv_w), row),
            pl.BlockSpec((heads, tm), col),
        ] + cast_specs,
        out_shape=[
            jax.ShapeDtypeStruct((attn_w, n), BF16),
            jax.ShapeDtypeStruct((n, attn_w), BF16),
            jax.ShapeDtypeStruct((attn_w, n), BF16),
            jax.ShapeDtypeStruct((n, conv_w), F32),
            jax.ShapeDtypeStruct((heads, n), F32),
        ] + [jax.ShapeDtypeStruct(a.shape, BF16) for a in to_cast],
        compiler_params=pltpu.CompilerParams(
            dimension_semantics=("arbitrary",), vmem_limit_bytes=VMEM_LIMIT),
        name="in_proj",
    )(x2, g, wqv_t, wrest, bf, *to_cast)
    return outs[:5], outs[5:]


def _cumsum_kernel(lf_ref, tri_ref, scat_ref, ones_ref, kaug_ref, crow_ref, *, blk, heads):
    assert heads == SUBLANES
    seq = lf_ref.shape[1]
    nblk = seq // blk
    tri_u = tri_ref[...]

    def stack3(x):
        terms = [t.astype(F32) for t in _split3(x)]
        return jnp.concatenate(terms + [jnp.zeros_like(x)], axis=0).astype(BF16)

    local = []
    for r in range(nblk):
        sums = jnp.dot(stack3(lf_ref[:, r * blk:(r + 1) * blk]), tri_u,
                       preferred_element_type=F32)
        local.append(sum(sums[t * heads:(t + 1) * heads, :] for t in range(N_SPLIT)))
    offset = jnp.zeros((heads, 1), F32)
    for r in range(nblk):
        c = local[r] + offset
        offset = c[:, blk - 1:blk]
        c = c * LOG2E
        crow_ref[0, :, r * blk:(r + 1) * blk] = c
        aug = ones_ref[...] + lax.dot_general(stack3(-c), scat_ref[...], TN_DIMS,
                                              preferred_element_type=F32)
        kaug_ref[0, r * blk:(r + 1) * blk, :] = aug.astype(BF16)


def _forget_cumsum(lf_t, b, *, blk):
    heads, n = lf_t.shape
    s = n // b
    tri = (jnp.arange(blk)[:, None] <= jnp.arange(blk)[None, :]).astype(BF16)
    lane = jnp.arange(LANES)
    row = jnp.arange(4 * heads)
    src_t, src_h = row[:, None] // heads, row[:, None] % heads
    scat = ((lane[None, :] == N_SPLIT + N_SPLIT * src_h + src_t) & (src_t < N_SPLIT)).astype(BF16)
    ones_row = (lane < N_SPLIT).astype(F32).reshape(1, LANES)
    const2 = lambda i: (0, 0)
    return pl.pallas_call(
        functools.partial(_cumsum_kernel, blk=blk, heads=heads),
        grid=(b,),
        in_specs=[
            pl.BlockSpec((heads, s), lambda i: (0, i)),
            pl.BlockSpec(tri.shape, const2),
            pl.BlockSpec(scat.shape, const2),
            pl.BlockSpec(ones_row.shape, const2),
        ],
        out_specs=[
            pl.BlockSpec((1, s, LANES), lambda i: (i, 0, 0)),
            pl.BlockSpec((1, heads, s), lambda i: (i, 0, 0)),
        ],
        out_shape=[
            jax.ShapeDtypeStruct((b, s, LANES), BF16),
            jax.ShapeDtypeStruct((b, heads, s), F32),
        ],
        compiler_params=pltpu.CompilerParams(dimension_semantics=("arbitrary",)),
        name="forget_cumsum",
    )(lf_t, tri, scat, ones_row)


def _allmax_sublanes(x):
    for shift in (4, 2, 1):
        x = jnp.maximum(x, pltpu.roll(x, shift, axis=0))
    return x


def _attn_kernel(qt_ref, k_ref, kaug_ref, vt_ref, crow_ref, g_ref, o_ref,
                 qrhs_ref, m_ref, l_ref, acc_ref, *, tq, tk, heads):
    n_q = k_ref.shape[1] // tq
    kt = tq // tk
    rowid = lax.broadcasted_iota(jnp.int32, (LANES, tq), 0)
    key_pos = lax.broadcasted_iota(jnp.int32, (tk, tq), 0)
    qry_pos = lax.broadcasted_iota(jnp.int32, (tk, tq), 1)
    causal = [key_pos + d * tk <= qry_pos for d in range(kt)]

    def prepare(i, slot):
        qcols = slice(i * tq, (i + 1) * tq)
        for h in range(heads):
            pair, e = divmod(h, 2)
            q_pair = qt_ref[pair * LANES:(pair + 1) * LANES, qcols]
            own = (rowid >= HEAD_DIM * e) & (rowid < HEAD_DIM * (e + 1))
            qrhs_ref[slot, h, 0:LANES, :] = jnp.where(own, q_pair, jnp.zeros_like(q_pair))
            c_hi, c_mid, c_lo = (t.astype(F32) for t in _split3(crow_ref[0, h:h + 1, qcols]))
            own_lanes = (rowid >= N_SPLIT * (h + 1)) & (rowid < N_SPLIT * (h + 2))
            aug = jnp.where(rowid == 0, c_hi,
                            jnp.where(rowid == 1, c_mid,
                                      jnp.where(rowid == 2, c_lo,
                                                jnp.where(own_lanes, 1.0, 0.0))))
            qrhs_ref[slot, h, LANES:2 * LANES, :] = aug.astype(BF16)
        m_ref[slot] = jnp.full(m_ref.shape[1:], NEG_BIG, F32)
        l_ref[slot] = jnp.zeros(l_ref.shape[1:], F32)
        acc_ref[slot] = jnp.zeros(acc_ref.shape[1:], F32)

    ones_rows = jnp.ones((2 * SUBLANES, tk), BF16)

    def scores(i, j, h):
        keys = slice(j * tk, (j + 1) * tk)
        pair = h // 2
        k_lhs = jnp.concatenate(
            [k_ref[0, keys, pair * LANES:(pair + 1) * LANES], kaug_ref[0, keys, :]], axis=1)
        return jnp.dot(k_lhs, qrhs_ref[i % 2, h], preferred_element_type=F32)

    def absorb(i, j, h, s_t):
        slot = i % 2
        keys = slice(j * tk, (j + 1) * tk)
        if j >= i * kt:
            s_t = jnp.where(causal[j - i * kt], s_t, NEG_BIG)
        s3 = s_t.reshape(tk // SUBLANES, SUBLANES, tq)
        m_old = m_ref[slot, h]
        m_new = jnp.maximum(m_old, _allmax_sublanes(jnp.max(s3, axis=0)))
        alpha = jnp.exp2(m_old - m_new)
        p = jnp.exp2(s3 - m_new[None]).reshape(tk, tq).astype(BF16)
        rows = slice(h * HEAD_DIM, (h + 1) * HEAD_DIM)
        v_aug = jnp.concatenate([vt_ref[rows, keys], ones_rows], axis=0)
        pv = jnp.dot(v_aug, p, preferred_element_type=F32)
        l_ref[slot, h] = alpha * l_ref[slot, h] + pv[HEAD_DIM:HEAD_DIM + SUBLANES, :]
        acc3 = acc_ref[slot, rows, :].reshape(HEAD_DIM // SUBLANES, SUBLANES, tq)
        pv3 = pv[:HEAD_DIM, :].reshape(HEAD_DIM // SUBLANES, SUBLANES, tq)
        acc_ref[slot, rows, :] = (alpha[None] * acc3 + pv3).reshape(HEAD_DIM, tq)
        m_ref[slot, h] = m_new

    def finish(i, slot):
        qcols = slice(i * tq, (i + 1) * tq)
        for h in range(heads):
            rows = slice(h * HEAD_DIM, (h + 1) * HEAD_DIM)
            acc3 = acc_ref[slot, rows, :].reshape(HEAD_DIM // SUBLANES, SUBLANES, tq)
            o = (acc3 / l_ref[slot, h][None]).reshape(HEAD_DIM, tq)
            ms = jnp.mean(o * o, axis=0, keepdims=True)
            o_ref[rows, qcols] = (o * lax.rsqrt(ms + EPS) * g_ref[rows, :]).astype(o_ref.dtype)

    chains = [(i, j, h) for i in range(n_q) for j in range((i + 1) * kt) for h in range(heads)]
    lookahead = 12
    assert lookahead <= 2 * heads
    prepared = set()
    pending = {}

    def issue(p):
        if p < len(chains):
            i = chains[p][0]
            if i not in prepared:
                prepare(i, i % 2)
                prepared.add(i)
            pending[p] = scores(*chains[p])

    for p in range(lookahead):
        issue(p)
    for p, (i, j, h) in enumerate(chains):
        absorb(i, j, h, pending.pop(p))
        if j == (i + 1) * kt - 1 and h == heads - 1:
            finish(i, i % 2)
        issue(p + lookahead)


def _attention(q_t, k3, kaug, v_t, c_row, g_col, *, tq, tk):
    b, s, w = k3.shape
    heads = w // HEAD_DIM
    return pl.pallas_call(
        functools.partial(_attn_kernel, tq=tq, tk=tk, heads=heads),
        grid=(b,),
        in_specs=[
            pl.BlockSpec((w, s), lambda bi: (0, bi)),
            pl.BlockSpec((1, s, w), lambda bi: (bi, 0, 0)),
            pl.BlockSpec((1, s, LANES), lambda bi: (bi, 0, 0)),
            pl.BlockSpec((w, s), lambda bi: (0, bi)),
            pl.BlockSpec((1, heads, s), lambda bi: (bi, 0, 0)),
            pl.BlockSpec((w, 1), lambda bi: (0, 0)),
        ],
        out_specs=pl.BlockSpec((w, s), lambda bi: (0, bi)),
        out_shape=jax.ShapeDtypeStruct((w, b * s), BF16),
        scratch_shapes=[
            pltpu.VMEM((2, heads, 2 * LANES, tq), BF16),
            pltpu.VMEM((2, heads, SUBLANES, tq), F32),
            pltpu.VMEM((2, heads, SUBLANES, tq), F32),
            pltpu.VMEM((2, w, tq), F32),
        ],
        compiler_params=pltpu.CompilerParams(
            dimension_semantics=("arbitrary",), vmem_limit_bytes=VMEM_LIMIT),
        name="fox_attention",
    )(q_t, k3, kaug, v_t, c_row, g_col)


def _depthwise_strips(u_ref, halo_ref, dw_ref, db_ref, pad_ref, dwout_ref, *, first_in_seq, sub_rows):
    tm, width = u_ref.shape
    halo = halo_ref.shape[0]
    prev = halo_ref[...]
    pad_ref[0:halo, :] = jnp.where(first_in_seq, jnp.zeros_like(prev), prev)
    pad_ref[halo:, :] = u_ref[...]
    lead = halo - (CONV_KERNEL - 1)
    window = sub_rows + halo

    def strip(sb, cb):
        lanes = slice(cb * LANES, (cb + 1) * LANES)
        x_win = pad_ref[sb * sub_rows:sb * sub_rows + window, lanes]
        acc = jnp.zeros((sub_rows, LANES), F32) + db_ref[:, lanes]
        for res in range(SUBLANES):
            taps = [t for t in range(CONV_KERNEL) if (lead + t) % SUBLANES == res]
            shifted = x_win if res == 0 else pltpu.roll(x_win, window - res, axis=0)
            for t in taps:
                a0 = (lead + t) - res
                acc = acc + shifted[a0:a0 + sub_rows, :] * dw_ref[t:t + 1, lanes]
        dwout_ref[sb * sub_rows:(sb + 1) * sub_rows, lanes] = acc
        return jnp.sum(acc, axis=0, keepdims=True)

    return [functools.partial(strip, sb, cb)
            for sb in range(tm // sub_rows) for cb in range(width // LANES)]


def _norm_swish(x, lg_ref, lb_ref):
    mu = jnp.mean(x, axis=-1, keepdims=True)
    xc = x - mu
    var = jnp.mean(xc * xc, axis=-1, keepdims=True)
    y = xc * lax.rsqrt(var + EPS) * lg_ref[...] + lb_ref[...]
    return y * (1.0 / (1.0 + jnp.exp(-y)))


def _post_kernel(x_ref, at_ref, u_ref, halo_ref, dw_ref, db_ref, lg_ref, lb_ref, pw_ref, pb_ref,
                 og_ref, gsum_ref, woa_ref, woc_ref, gf_ref, wup_ref, wdn_ref, gl_ref,
                 o_ref, pad_ref, dwout_ref, y_ref,
                 *, tiles_per_seq, n_tiles, sub_rows, part_rows, ff_chunks, apply_final_norm):
    g = pl.program_id(0)

    @pl.when(g == 0)
    def _():
        y_ref[...] = jnp.zeros(y_ref.shape, y_ref.dtype)

    tile = jnp.minimum(g, n_tiles - 1)
    first_in_seq = lax.rem(tile, tiles_per_seq) == 0
    strips = _depthwise_strips(u_ref, halo_ref, dw_ref, db_ref, pad_ref, dwout_ref,
                               first_in_seq=first_in_seq, sub_rows=sub_rows)
    tm = o_ref.shape[0]
    ff = wup_ref.shape[1] // ff_chunks
    bounds = [round(w * len(strips) / (ff_chunks - 1)) for w in range(ff_chunks)]
    norm_rows = tm // 4
    top = 2 * SUBLANES

    def norm_block(r):
        rows = slice(r * norm_rows, (r + 1) * norm_rows)
        y = _norm_swish(dwout_ref[rows, :], lg_ref, lb_ref)
        y_ref[rows, :] = y.astype(y_ref.dtype)
        t = jnp.sum(y, axis=0, keepdims=True)
        return sum(t[:, i * LANES:(i + 1) * LANES] for i in range(y.shape[1] // LANES))

    def tied(lhs, work):
        tokens = [item() for item in work]
        t = sum(tokens[1:], tokens[0])
        corner = jnp.where(g >= 0, lhs[:top, :LANES], jnp.broadcast_to(t, (top, LANES)).astype(BF16))
        return jnp.concatenate(
            [jnp.concatenate([corner, lhs[:top, LANES:]], axis=1), lhs[top:, :]], axis=0)

    halves = [slice(i * part_rows, (i + 1) * part_rows) for i in range(tm // part_rows)]
    z = [jnp.dot(y_ref[r, :], pw_ref[...], preferred_element_type=F32) + pb_ref[...] for r in halves]
    x_attn = [x_ref[r, :] + lax.dot_general(at_ref[:, r], woa_ref[...], TN_DIMS,
                                            preferred_element_type=F32) for r in halves]
    h = []
    for i, r in enumerate(halves):
        ms = jnp.dot((z[i] * z[i]).astype(BF16), gsum_ref[...],
                     preferred_element_type=F32) * (1.0 / HEAD_DIM)
        conv = (z[i] * lax.rsqrt(ms + EPS) * og_ref[...]).astype(BF16)
        x1 = x_attn[i] + jnp.dot(conv, woc_ref[...], preferred_element_type=F32)
        o_ref[r, :] = x1
        h.append(_rms(x1, gf_ref[...]).astype(BF16))
    for c in range(ff_chunks):
        cols = slice(c * ff, (c + 1) * ff)
        lhs = [tied(h[0], strips[bounds[c - 1]:bounds[c]]) if c else h[0]] + h[1:]
        a2 = []
        for h_i in lhs:
            a = jnp.maximum(jnp.dot(h_i, wup_ref[:, cols], preferred_element_type=F32), 0.0)
            a2.append((a * a).astype(BF16))
        if c == ff_chunks - 1:
            a2[-1] = tied(a2[-1], [functools.partial(norm_block, r) for r in range(tm // norm_rows)])
        for r, a2_i in zip(halves, a2):
            o_ref[r, :] += jnp.dot(a2_i, wdn_ref[cols, :], preferred_element_type=F32)
    if apply_final_norm:
        for r in halves:
            o_ref[r, :] = _rms(o_ref[r, :], gl_ref[...])


def _post(x2, attn_t, u, dw, db, lg, lb, pw, pb, og, gsum, woa, woc, gf, wup, wdn, gl,
          *, tm, seq, apply_final_norm):
    n, d = x2.shape
    w = u.shape[1]
    halo = 32
    n_tiles = n // tm
    const = lambda g: (0, 0)
    prev_tile = lambda g: (jnp.maximum(g - 1, 0), 0)
    cur_tile = lambda g: (jnp.minimum(g, n_tiles - 1), 0)
    halo_blk = lambda g: (jnp.maximum(jnp.minimum(g, n_tiles - 1) * (tm // halo) - 1, 0), 0)
    vec = pl.BlockSpec((1, w), const)
    resident = lambda a: pl.BlockSpec(a.shape, const, pipeline_mode=pl.Buffered(1))
    return pl.pallas_call(
        functools.partial(_post_kernel, tiles_per_seq=seq // tm, n_tiles=n_tiles,
                          sub_rows=CONV_STRIP_ROWS, part_rows=POST_PART_ROWS, ff_chunks=POST_FF_CHUNKS,
                          apply_final_norm=apply_final_norm),
        grid=(n_tiles + 1,),
        in_specs=[
            pl.BlockSpec((tm, d), prev_tile),
            pl.BlockSpec((attn_t.shape[0], tm), lambda g: (0, jnp.maximum(g - 1, 0))),
            pl.BlockSpec((tm, w), cur_tile),
            pl.BlockSpec((halo, w), halo_blk),
            pl.BlockSpec(dw.shape, const),
            vec, vec, vec,
            pl.BlockSpec(pw.shape, const),
            vec, vec,
            pl.BlockSpec(gsum.shape, const),
            resident(woa),
            resident(woc),
            pl.BlockSpec((1, d), const),
            resident(wup),
            resident(wdn),
            pl.BlockSpec((1, d), const),
        ],
        out_specs=pl.BlockSpec((tm, d), prev_tile),
        out_shape=jax.ShapeDtypeStruct((n, d), F32),
        scratch_shapes=[
            pltpu.VMEM((tm + halo, w), F32),
            pltpu.VMEM((tm, w), F32),
            pltpu.VMEM((tm, w), BF16),
        ],
        compiler_params=pltpu.CompilerParams(
            dimension_semantics=("arbitrary",), vmem_limit_bytes=VMEM_LIMIT),
        name="conv_out_proj_ffn",
    )(x2, attn_t, u, u, dw, db, lg, lb, pw, pb, og, gsum, woa, woc, gf, wup, wdn, gl)


def _group_sum_matrix(width):
    idx = jnp.arange(width) // HEAD_DIM
    return (idx[:, None] == idx[None, :]).astype(BF16)


def kernel(x, norm_mix_g, w_in, b_forget, conv_dw_w, conv_dw_b, conv_ln_g, conv_ln_b,
           w_conv_pw, b_conv_pw, attn_out_g, conv_out_g, w_out, norm_ffn_g,
           w_ffn_up, w_ffn_down, norm_final_g):
    b, s, d = x.shape
    depth = w_in.shape[0]
    heads = b_forget.shape[1]
    attn_w = heads * HEAD_DIM
    conv_w = conv_dw_w.shape[2]
    n = b * s
    x2 = x.reshape(n, d)
    gsum_conv = _group_sum_matrix(conv_w)
    for l in range(depth):
        w = w_in[l]
        wq, wk, wv = w[:, :attn_w], w[:, attn_w:2 * attn_w], w[:, 2 * attn_w:3 * attn_w]
        wf = jnp.pad(w[:, 3 * attn_w + 2 * conv_w:], ((0, 0), (0, 2 * SUBLANES - heads)))
        wqvf_t = jnp.concatenate([wq, wv, wf], axis=1).T.astype(BF16)
        wrest = jnp.concatenate([wk, w[:, 3 * attn_w:3 * attn_w + 2 * conv_w]], axis=1).astype(BF16)
        (q_t, k, v_t, u, lf_t), (wo, wup, wdn, wpw) = _in_proj(
            x2, norm_mix_g[l].reshape(1, d), wqvf_t, wrest, b_forget[l].reshape(heads, 1),
            [w_out[l], w_ffn_up[l], w_ffn_down[l], w_conv_pw[l]],
            attn_w=attn_w, conv_w=conv_w, tm=IN_PROJ_ROWS)

        kaug, c_row = _forget_cumsum(lf_t, b, blk=CUMSUM_BLOCK)
        attn_t = _attention(q_t, k.reshape(b, s, attn_w), kaug, v_t, c_row,
                            attn_out_g[l].reshape(attn_w, 1), tq=ATTN_TILE, tk=ATTN_TILE)

        vec = lambda a: a.reshape(1, conv_w)
        x2 = _post(x2, attn_t, u, conv_dw_w[l], vec(conv_dw_b[l]), vec(conv_ln_g[l]),
                   vec(conv_ln_b[l]), wpw, vec(b_conv_pw[l]),
                   vec(conv_out_g[l]), gsum_conv, wo[:attn_w], wo[attn_w:],
                   norm_ffn_g[l].reshape(1, d), wup, wdn, norm_final_g.reshape(1, d),
                   tm=POST_ROWS, seq=s, apply_final_norm=(l == depth - 1))
    return x2.reshape(b, s, d)
```

```python
import functools

import jax
import jax.numpy as jnp
from jax import lax
from jax.experimental import pallas as pl
from jax.experimental.pallas import tpu as pltpu

F32 = jnp.float32
BF16 = jnp.bfloat16

HEAD_DIM = 64
CONV_KERNEL = 31
EPS = 1e-6
LANES = 128
SUBLANES = 8
VMEM_LIMIT = 56 * 1024 * 1024
NEG_BIG = -1e30
LOG2E = 1.4426950408889634
N_SPLIT = 3
NT_DIMS = (((1,), (1,)), ((), ()))
TN_DIMS = (((0,), (0,)), ((), ()))

IN_PROJ_ROWS = 1024
IN_PROJ_PARTS = 4
CUMSUM_BLOCK = 256
ATTN_TILE = 256
POST_ROWS = 512
POST_PART_ROWS = 256
POST_FF_CHUNKS = 4
CONV_STRIP_ROWS = 64


def _rms(x, g):
    ms = jnp.mean(x * x, axis=-1, keepdims=True)
    return x * lax.rsqrt(ms + EPS) * g


def _split3(x):
    hi = x.astype(BF16)
    r = x - hi.astype(F32)
    mid = r.astype(BF16)
    lo = (r - mid.astype(F32)).astype(BF16)
    return hi, mid, lo


def _in_proj_kernel(x_ref, g_ref, wqvf_t_ref, wrest_ref, bf_ref, *refs, attn_w, conv_w, parts, n_cast):
    cast_in, (qt_ref, k_ref, vt_ref, u_ref, lf_ref), cast_out = (
        refs[:n_cast], refs[n_cast:n_cast + 5], refs[n_cast + 5:])
    for src, dst in zip(cast_in, cast_out):
        dst[...] = src[...].astype(dst.dtype)
    tm = x_ref.shape[0]
    heads = lf_ref.shape[0]
    halves = [slice(i * (tm // parts), (i + 1) * (tm // parts)) for i in range(parts)]
    hs = [_rms(x_ref[r, :], g_ref[...]).astype(BF16) for r in halves]
    for r, h in zip(halves, hs):
        qvf_t = lax.dot_general(wqvf_t_ref[...], h, NT_DIMS, preferred_element_type=F32)
        qt_ref[:, r] = (qvf_t[:attn_w] * (LOG2E * HEAD_DIM ** -0.5)).astype(BF16)
        vt_ref[:, r] = qvf_t[attn_w:2 * attn_w].astype(BF16)
        fl = qvf_t[2 * attn_w:2 * attn_w + heads] + bf_ref[...]
        lf_ref[:, r] = jnp.minimum(fl, 0.0) - jnp.log1p(jnp.exp(-jnp.abs(fl)))
    for r, h in zip(halves, hs):
        rest = jnp.dot(h, wrest_ref[...], preferred_element_type=F32)
        k_ref[r, :] = rest[:, :attn_w].astype(BF16)
        ga = rest[:, attn_w:attn_w + conv_w]
        gb = rest[:, attn_w + conv_w:]
        u_ref[r, :] = ga * (1.0 / (1.0 + jnp.exp(-gb)))


def _in_proj(x2, g, wqv_t, wrest, bf, to_cast, *, attn_w, conv_w, tm):
    n, d = x2.shape
    heads = bf.shape[0]
    steps = n // tm
    const = lambda i: (0, 0)
    row = lambda i: (i, 0)
    col = lambda i: (0, i)
    cast_specs = [pl.BlockSpec((a.shape[0] // steps, a.shape[1]), row) for a in to_cast]
    assert all(a.shape[0] % (steps * 2 * SUBLANES) == 0 for a in to_cast)
    outs = pl.pallas_call(
        functools.partial(_in_proj_kernel, attn_w=attn_w, conv_w=conv_w, parts=IN_PROJ_PARTS,
                          n_cast=len(to_cast)),
        grid=(steps,),
        in_specs=[
            pl.BlockSpec((tm, d), row),
            pl.BlockSpec((1, d), const),
            pl.BlockSpec(wqv_t.shape, const),
            pl.BlockSpec(wrest.shape, const),
            pl.BlockSpec(bf.shape, const),
        ] + cast_specs,
        out_specs=[
            pl.BlockSpec((attn_w, tm), col),
            pl.BlockSpec((tm, attn_w), row),
            pl.BlockSpec((attn_w, tm), col),
            pl.BlockSpec((tm, conv_w), row),
            pl.BlockSpec((heads, tm), col),
        ] + cast_specs,
        out_shape=[
            jax.ShapeDtypeStruct((attn_w, n), BF16),
            jax.ShapeDtypeStruct((n, attn_w), BF16),
            jax.ShapeDtypeStruct((attn_w, n), BF16),
            jax.ShapeDtypeStruct((n, conv_w), F32),
            jax.ShapeDtypeStruct((heads, n), F32),
        ] + [jax.ShapeDtypeStruct(a.shape, BF16) for a in to_cast],
        compiler_params=pltpu.CompilerParams(
            dimension_semantics=("arbitrary",), vmem_limit_bytes=VMEM_LIMIT),
        name="in_proj",
    )(x2, g, wqv_t, wrest, bf, *to_cast)
    return outs[:5], outs[5:]


def _cumsum_kernel(lf_ref, tri_ref, scat_ref, ones_ref, kaug_ref, crow_ref, *, blk, heads):
    assert heads == SUBLANES
    seq = lf_ref.shape[1]
    nblk = seq // blk
    tri_u = tri_ref[...]

    def stack3(x):
        terms = [t.astype(F32) for t in _split3(x)]
        return jnp.concatenate(terms + [jnp.zeros_like(x)], axis=0).astype(BF16)

    local = []
    for r in range(nblk):
        sums = jnp.dot(stack3(lf_ref[:, r * blk:(r + 1) * blk]), tri_u,
                       preferred_element_type=F32)
        local.append(sum(sums[t * heads:(t + 1) * heads, :] for t in range(N_SPLIT)))
    offset = jnp.zeros((heads, 1), F32)
    for r in range(nblk):
        c = local[r] + offset
        offset = c[:, blk - 1:blk]
        c = c * LOG2E
        crow_ref[0, :, r * blk:(r + 1) * blk] = c
        aug = ones_ref[...] + lax.dot_general(stack3(-c), scat_ref[...], TN_DIMS,
                                              preferred_element_type=F32)
        kaug_ref[0, r * blk:(r + 1) * blk, :] = aug.astype(BF16)


def _forget_cumsum(lf_t, b, *, blk):
    heads, n = lf_t.shape
    s = n // b
    tri = (jnp.arange(blk)[:, None] <= jnp.arange(blk)[None, :]).astype(BF16)
    lane = jnp.arange(LANES)
    row = jnp.arange(4 * heads)
    src_t, src_h = row[:, None] // heads, row[:, None] % heads
    scat = ((lane[None, :] == N_SPLIT + N_SPLIT * src_h + src_t) & (src_t < N_SPLIT)).astype(BF16)
    ones_row = (lane < N_SPLIT).astype(F32).reshape(1, LANES)
    const2 = lambda i: (0, 0)
    return pl.pallas_call(
        functools.partial(_cumsum_kernel, blk=blk, heads=heads),
        grid=(b,),
        in_specs=[
            pl.BlockSpec((heads, s), lambda i: (0, i)),
            pl.BlockSpec(tri.shape, const2),
            pl.BlockSpec(scat.shape, const2),
            pl.BlockSpec(ones_row.shape, const2),
        ],
        out_specs=[
            pl.BlockSpec((1, s, LANES), lambda i: (i, 0, 0)),
            pl.BlockSpec((1, heads, s), lambda i: (i, 0, 0)),
        ],
        out_shape=[
            jax.ShapeDtypeStruct((b, s, LANES), BF16),
            jax.ShapeDtypeStruct((b, heads, s), F32),
        ],
        compiler_params=pltpu.CompilerParams(dimension_semantics=("arbitrary",)),
        name="forget_cumsum",
    )(lf_t, tri, scat, ones_row)


def _allmax_sublanes(x):
    for shift in (4, 2, 1):
        x = jnp.maximum(x, pltpu.roll(x, shift, axis=0))
    return x


def _attn_kernel(qt_ref, k_ref, kaug_ref, vt_ref, crow_ref, g_ref, o_ref,
                 qrhs_ref, m_ref, l_ref, acc_ref, *, tq, tk, heads):
    n_q = k_ref.shape[1] // tq
    kt = tq // tk
    rowid = lax.broadcasted_iota(jnp.int32, (LANES, tq), 0)
    key_pos = lax.broadcasted_iota(jnp.int32, (tk, tq), 0)
    qry_pos = lax.broadcasted_iota(jnp.int32, (tk, tq), 1)
    causal = [key_pos + d * tk <= qry_pos for d in range(kt)]

    def prepare(i, slot):
        qcols = slice(i * tq, (i + 1) * tq)
        for h in range(heads):
            pair, e = divmod(h, 2)
            q_pair = qt_ref[pair * LANES:(pair + 1) * LANES, qcols]
            own = (rowid >= HEAD_DIM * e) & (rowid < HEAD_DIM * (e + 1))
            qrhs_ref[slot, h, 0:LANES, :] = jnp.where(own, q_pair, jnp.zeros_like(q_pair))
            c_hi, c_mid, c_lo = (t.astype(F32) for t in _split3(crow_ref[0, h:h + 1, qcols]))
            own_lanes = (rowid >= N_SPLIT * (h + 1)) & (rowid < N_SPLIT * (h + 2))
            aug = jnp.where(rowid == 0, c_hi,
                            jnp.where(rowid == 1, c_mid,
                                      jnp.where(rowid == 2, c_lo,
                                                jnp.where(own_lanes, 1.0, 0.0))))
            qrhs_ref[slot, h, LANES:2 * LANES, :] = aug.astype(BF16)
        m_ref[slot] = jnp.full(m_ref.shape[1:], NEG_BIG, F32)
        l_ref[slot] = jnp.zeros(l_ref.shape[1:], F32)
        acc_ref[slot] = jnp.zeros(acc_ref.shape[1:], F32)

    ones_rows = jnp.ones((2 * SUBLANES, tk), BF16)

    def scores(i, j, h):
        keys = slice(j * tk, (j + 1) * tk)
        pair = h // 2
        k_lhs = jnp.concatenate(
            [k_ref[0, keys, pair * LANES:(pair + 1) * LANES], kaug_ref[0, keys, :]], axis=1)
        return jnp.dot(k_lhs, qrhs_ref[i % 2, h], preferred_element_type=F32)

    def absorb(i, j, h, s_t):
        slot = i % 2
        keys = slice(j * tk, (j + 1) * tk)
        if j >= i * kt:
            s_t = jnp.where(causal[j - i * kt], s_t, NEG_BIG)
        s3 = s_t.reshape(tk // SUBLANES, SUBLANES, tq)
        m_old = m_ref[slot, h]
        m_new = jnp.maximum(m_old, _allmax_sublanes(jnp.max(s3, axis=0)))
        alpha = jnp.exp2(m_old - m_new)
        p = jnp.exp2(s3 - m_new[None]).reshape(tk, tq).astype(BF16)
        rows = slice(h * HEAD_DIM, (h + 1) * HEAD_DIM)
        v_aug = jnp.concatenate([vt_ref[rows, keys], ones_rows], axis=0)
        pv = jnp.dot(v_aug, p, preferred_element_type=F32)
        l_ref[slot, h] = alpha * l_ref[slot, h] + pv[HEAD_DIM:HEAD_DIM + SUBLANES, :]
        acc3 = acc_ref[slot, rows, :].reshape(HEAD_DIM // SUBLANES, SUBLANES, tq)
        pv3 = pv[:HEAD_DIM, :].reshape(HEAD_DIM // SUBLANES, SUBLANES, tq)
        acc_ref[slot, rows, :] = (alpha[None] * acc3 + pv3).reshape(HEAD_DIM, tq)
        m_ref[slot, h] = m_new

    def finish(i, slot):
        qcols = slice(i * tq, (i + 1) * tq)
        for h in range(heads):
            rows = slice(h * HEAD_DIM, (h + 1) * HEAD_DIM)
            acc3 = acc_ref[slot, rows, :].reshape(HEAD_DIM // SUBLANES, SUBLANES, tq)
            o = (acc3 / l_ref[slot, h][None]).reshape(HEAD_DIM, tq)
            ms = jnp.mean(o * o, axis=0, keepdims=True)
            o_ref[rows, qcols] = (o * lax.rsqrt(ms + EPS) * g_ref[rows, :]).astype(o_ref.dtype)

    chains = [(i, j, h) for i in range(n_q) for j in range((i + 1) * kt) for h in range(heads)]
    lookahead = 12
    assert lookahead <= 2 * heads
    prepared = set()
    pending = {}

    def issue(p):
        if p < len(chains):
            i = chains[p][0]
            if i not in prepared:
                prepare(i, i % 2)
                prepared.add(i)
            pending[p] = scores(*chains[p])

    for p in range(lookahead):
        issue(p)
    for p, (i, j, h) in enumerate(chains):
        absorb(i, j, h, pending.pop(p))
        if j == (i + 1) * kt - 1 and h == heads - 1:
            finish(i, i % 2)
        issue(p + lookahead)


def _attention(q_t, k3, kaug, v_t, c_row, g_col, *, tq, tk):
    b, s, w = k3.shape
    heads = w // HEAD_DIM
    return pl.pallas_call(
        functools.partial(_attn_kernel, tq=tq, tk=tk, heads=heads),
        grid=(b,),
        in_specs=[
            pl.BlockSpec((w, s), lambda bi: (0, bi)),
            pl.BlockSpec((1, s, w), lambda bi: (bi, 0, 0)),
            pl.BlockSpec((1, s, LANES), lambda bi: (bi, 0, 0)),
            pl.BlockSpec((w, s), lambda bi: (0, bi)),
            pl.BlockSpec((1, heads, s), lambda bi: (bi, 0, 0)),
            pl.BlockSpec((w, 1), lambda bi: (0, 0)),
        ],
        out_specs=pl.BlockSpec((w, s), lambda bi: (0, bi)),
        out_shape=jax.ShapeDtypeStruct((w, b * s), BF16),
        scratch_shapes=[
            pltpu.VMEM((2, heads, 2 * LANES, tq), BF16),
            pltpu.VMEM((2, heads, SUBLANES, tq), F32),
            pltpu.VMEM((2, heads, SUBLANES, tq), F32),
            pltpu.VMEM((2, w, tq), F32),
        ],
        compiler_params=pltpu.CompilerParams(
            dimension_semantics=("arbitrary",), vmem_limit_bytes=VMEM_LIMIT),
        name="fox_attention",
    )(q_t, k3, kaug, v_t, c_row, g_col)


def _depthwise_strips(u_ref, halo_ref, dw_ref, db_ref, pad_ref, dwout_ref, *, first_in_seq, sub_rows):
    tm, width = u_ref.shape
    halo = halo_ref.shape[0]
    prev = halo_ref[...]
    pad_ref[0:halo, :] = jnp.where(first_in_seq, jnp.zeros_like(prev), prev)
    pad_ref[halo:, :] = u_ref[...]
    lead = halo - (CONV_KERNEL - 1)
    window = sub_rows + halo

    def strip(sb, cb):
        lanes = slice(cb * LANES, (cb + 1) * LANES)
        x_win = pad_ref[sb * sub_rows:sb * sub_rows + window, lanes]
        acc = jnp.zeros((sub_rows, LANES), F32) + db_ref[:, lanes]
        for res in range(SUBLANES):
            taps = [t for t in range(CONV_KERNEL) if (lead + t) % SUBLANES == res]
            shifted = x_win if res == 0 else pltpu.roll(x_win, window - res, axis=0)
            for t in taps:
                a0 = (lead + t) - res
                acc = acc + shifted[a0:a0 + sub_rows, :] * dw_ref[t:t + 1, lanes]
        dwout_ref[sb * sub_rows:(sb + 1) * sub_rows, lanes] = acc
        return jnp.sum(acc, axis=0, keepdims=True)

    return [functools.partial(strip, sb, cb)
            for sb in range(tm // sub_rows) for cb in range(width // LANES)]


def _norm_swish(x, lg_ref, lb_ref):
    mu = jnp.mean(x, axis=-1, keepdims=True)
    xc = x - mu
    var = jnp.mean(xc * xc, axis=-1, keepdims=True)
    y = xc * lax.rsqrt(var + EPS) * lg_ref[...] + lb_ref[...]
    return y * (1.0 / (1.0 + jnp.exp(-y)))


def _post_kernel(x_ref, at_ref, u_ref, halo_ref, dw_ref, db_ref, lg_ref, lb_ref, pw_ref, pb_ref,
                 og_ref, gsum_ref, woa_ref, woc_ref, gf_ref, wup_ref, wdn_ref, gl_ref,
                 o_ref, pad_ref, dwout_ref, y_ref,
                 *, tiles_per_seq, n_tiles, sub_rows, part_rows, ff_chunks, apply_final_norm):
    g = pl.program_id(0)

    @pl.when(g == 0)
    def _():
        y_ref[...] = jnp.zeros(y_ref.shape, y_ref.dtype)

    tile = jnp.minimum(g, n_tiles - 1)
    first_in_seq = lax.rem(tile, tiles_per_seq) == 0
    strips = _depthwise_strips(u_ref, halo_ref, dw_ref, db_ref, pad_ref, dwout_ref,
                               first_in_seq=first_in_seq, sub_rows=sub_rows)
    tm = o_ref.shape[0]
    ff = wup_ref.shape[1] // ff_chunks
    bounds = [round(w * len(strips) / (ff_chunks - 1)) for w in range(ff_chunks)]
    norm_rows = tm // 4
    top = 2 * SUBLANES

    def norm_block(r):
        rows = slice(r * norm_rows, (r + 1) * norm_rows)
        y = _norm_swish(dwout_ref[rows, :], lg_ref, lb_ref)
        y_ref[rows, :] = y.astype(y_ref.dtype)
        t = jnp.sum(y, axis=0, keepdims=True)
        return sum(t[:, i * LANES:(i + 1) * LANES] for i in range(y.shape[1] // LANES))

    def tied(lhs, work):
        tokens = [item() for item in work]
        t = sum(tokens[1:], tokens[0])
        corner = jnp.where(g >= 0, lhs[:top, :LANES], jnp.broadcast_to(t, (top, LANES)).astype(BF16))
        return jnp.concatenate(
            [jnp.concatenate([corner, lhs[:top, LANES:]], axis=1), lhs[top:, :]], axis=0)

    halves = [slice(i * part_rows, (i + 1) * part_rows) for i in range(tm // part_rows)]
    z = [jnp.dot(y_ref[r, :], pw_ref[...], preferred_element_type=F32) + pb_ref[...] for r in halves]
    x_attn = [x_ref[r, :] + lax.dot_general(at_ref[:, r], woa_ref[...], TN_DIMS,
                                            preferred_element_type=F32) for r in halves]
    h = []
    for i, r in enumerate(halves):
        ms = jnp.dot((z[i] * z[i]).astype(BF16), gsum_ref[...],
                     preferred_element_type=F32) * (1.0 / HEAD_DIM)
        conv = (z[i] * lax.rsqrt(ms + EPS) * og_ref[...]).astype(BF16)
        x1 = x_attn[i] + jnp.dot(conv, woc_ref[...], preferred_element_type=F32)
        o_ref[r, :] = x1
        h.append(_rms(x1, gf_ref[...]).astype(BF16))
    for c in range(ff_chunks):
        cols = slice(c * ff, (c + 1) * ff)
        lhs = [tied(h[0], strips[bounds[c - 1]:bounds[c]]) if c else h[0]] + h[1:]
        a2 = []
        for h_i in lhs:
            a = jnp.maximum(jnp.dot(h_i, wup_ref[:, cols], preferred_element_type=F32), 0.0)
            a2.append((a * a).astype(BF16))
        if c == ff_chunks - 1:
            blocks = [functools.partial(norm_block, r) for r in range(tm // norm_rows)]
            a2[0] = tied(a2[0], blocks[:1])
            a2[-1] = tied(a2[-1], blocks[1:-1])
            blocks[-1]()
        for r, a2_i in zip(halves, a2):
            o_ref[r, :] += jnp.dot(a2_i, wdn_ref[cols, :], preferred_element_type=F32)
    if apply_final_norm:
        for r in halves:
            o_ref[r, :] = _rms(o_ref[r, :], gl_ref[...])


def _post(x2, attn_t, u, dw, db, lg, lb, pw, pb, og, gsum, woa, woc, gf, wup, wdn, gl,
          *, tm, seq, apply_final_norm):
    n, d = x2.shape
    w = u.shape[1]
    halo = 32
    n_tiles = n // tm
    const = lambda g: (0, 0)
    prev_tile = lambda g: (jnp.maximum(g - 1, 0), 0)
    cur_tile = lambda g: (jnp.minimum(g, n_tiles - 1), 0)
    halo_blk = lambda g: (jnp.maximum(jnp.minimum(g, n_tiles - 1) * (tm // halo) - 1, 0), 0)
    vec = pl.BlockSpec((1, w), const)
    resident = lambda a: pl.BlockSpec(a.shape, const, pipeline_mode=pl.Buffered(1))
    return pl.pallas_call(
        functools.partial(_post_kernel, tiles_per_seq=seq // tm, n_tiles=n_tiles,
                          sub_rows=CONV_STRIP_ROWS, part_rows=POST_PART_ROWS, ff_chunks=POST_FF_CHUNKS,
                          apply_final_norm=apply_final_norm),
        grid=(n_tiles + 1,),
        in_specs=[
            pl.BlockSpec((tm, d), prev_tile),
            pl.BlockSpec((attn_t.shape[0], tm), lambda g: (0, jnp.maximum(g - 1, 0))),
            pl.BlockSpec((tm, w), cur_tile),
            pl.BlockSpec((halo, w), halo_blk),
            pl.BlockSpec(dw.shape, const),
            vec, vec, vec,
            pl.BlockSpec(pw.shape, const),
            vec, vec,
            pl.BlockSpec(gsum.shape, const),
            resident(woa),
            resident(woc),
            pl.BlockSpec((1, d), const),
            resident(wup),
            resident(wdn),
            pl.BlockSpec((1, d), const),
        ],
        out_specs=pl.BlockSpec((tm, d), prev_tile),
        out_shape=jax.ShapeDtypeStruct((n, d), F32),
        scratch_shapes=[
            pltpu.VMEM((tm + halo, w), F32),
            pltpu.VMEM((tm, w), F32),
            pltpu.VMEM((tm, w), BF16),
        ],
        compiler_params=pltpu.CompilerParams(
            dimension_semantics=("arbitrary",), vmem_limit_bytes=VMEM_LIMIT),
        name="conv_out_proj_ffn",
    )(x2, attn_t, u, u, dw, db, lg, lb, pw, pb, og, gsum, woa, woc, gf, wup, wdn, gl)


def _group_sum_matrix(width):
    idx = jnp.arange(width) // HEAD_DIM
    return (idx[:, None] == idx[None, :]).astype(BF16)


def kernel(x, norm_mix_g, w_in, b_forget, conv_dw_w, conv_dw_b, conv_ln_g, conv_ln_b,
           w_conv_pw, b_conv_pw, attn_out_g, conv_out_g, w_out, norm_ffn_g,
           w_ffn_up, w_ffn_down, norm_final_g):
    b, s, d = x.shape
    depth = w_in.shape[0]
    heads = b_forget.shape[1]
    attn_w = heads * HEAD_DIM
    conv_w = conv_dw_w.shape[2]
    n = b * s
    x2 = x.reshape(n, d)
    gsum_conv = _group_sum_matrix(conv_w)
    for l in range(depth):
        w = w_in[l]
        wq, wk, wv = w[:, :attn_w], w[:, attn_w:2 * attn_w], w[:, 2 * attn_w:3 * attn_w]
        wf = jnp.pad(w[:, 3 * attn_w + 2 * conv_w:], ((0, 0), (0, 2 * SUBLANES - heads)))
        wqvf_t = jnp.concatenate([wq, wv, wf], axis=1).astype(BF16).T
        wrest = jnp.concatenate([wk, w[:, 3 * attn_w:3 * attn_w + 2 * conv_w]], axis=1).astype(BF16)
        (q_t, k, v_t, u, lf_t), (wo, wup, wdn, wpw) = _in_proj(
            x2, norm_mix_g[l].reshape(1, d), wqvf_t, wrest, b_forget[l].reshape(heads, 1),
            [w_out[l], w_ffn_up[l], w_ffn_down[l], w_conv_pw[l]],
            attn_w=attn_w, conv_w=conv_w, tm=IN_PROJ_ROWS)

        kaug, c_row = _forget_cumsum(lf_t, b, blk=CUMSUM_BLOCK)
        attn_t = _attention(q_t, k.reshape(b, s, attn_w), kaug, v_t, c_row,
                            attn_out_g[l].reshape(attn_w, 1), tq=ATTN_TILE, tk=ATTN_TILE)

        vec = lambda a: a.reshape(1, conv_w)
        x2 = _post(x2, attn_t, u, conv_dw_w[l], vec(conv_dw_b[l]), vec(conv_ln_g[l]),
                   vec(conv_ln_b[l]), wpw, vec(b_conv_pw[l]),
                   vec(conv_out_g[l]), gsum_conv, wo[:attn_w], wo[attn_w:],
                   norm_ffn_g[l].reshape(1, d), wup, wdn, norm_final_g.reshape(1, d),
                   tm=POST_ROWS, seq=s, apply_final_norm=(l == depth - 1))
    return x2.reshape(b, s, d)
```

```python
import functools

import jax
import jax.numpy as jnp
from jax import lax
from jax.experimental import pallas as pl
from jax.experimental.pallas import tpu as pltpu

F32 = jnp.float32
BF16 = jnp.bfloat16

HEAD_DIM = 64
CONV_KERNEL = 31
EPS = 1e-6
LANES = 128
SUBLANES = 8
VMEM_LIMIT = 56 * 1024 * 1024
NEG_BIG = -1e30
LOG2E = 1.4426950408889634
N_SPLIT = 3
NT_DIMS = (((1,), (1,)), ((), ()))
TN_DIMS = (((0,), (0,)), ((), ()))

IN_PROJ_ROWS = 2048
IN_PROJ_PARTS = 8
CUMSUM_BLOCK = 256
ATTN_TILE = 256
POST_ROWS = 512
POST_PART_ROWS = 256
POST_FF_CHUNKS = 4
CONV_STRIP_ROWS = 64


def _rms(x, g):
    ms = jnp.mean(x * x, axis=-1, keepdims=True)
    return x * lax.rsqrt(ms + EPS) * g


def _split3(x):
    hi = x.astype(BF16)
    r = x - hi.astype(F32)
    mid = r.astype(BF16)
    lo = (r - mid.astype(F32)).astype(BF16)
    return hi, mid, lo


def _in_proj_kernel(x_ref, g_ref, wqvf_t_ref, wrest_ref, bf_ref, *refs, attn_w, conv_w, parts, n_cast):
    cast_in, (qt_ref, k_ref, vt_ref, u_ref, lf_ref), cast_out = (
        refs[:n_cast], refs[n_cast:n_cast + 5], refs[n_cast + 5:])
    for src, dst in zip(cast_in, cast_out):
        dst[...] = src[...].astype(dst.dtype)
    tm = x_ref.shape[0]
    heads = lf_ref.shape[0]
    halves = [slice(i * (tm // parts), (i + 1) * (tm // parts)) for i in range(parts)]
    hs = [_rms(x_ref[r, :], g_ref[...]).astype(BF16) for r in halves]
    for r, h in zip(halves, hs):
        qvf_t = lax.dot_general(wqvf_t_ref[...], h, NT_DIMS, preferred_element_type=F32)
        qt_ref[:, r] = (qvf_t[:attn_w] * (LOG2E * HEAD_DIM ** -0.5)).astype(BF16)
        vt_ref[:, r] = qvf_t[attn_w:2 * attn_w].astype(BF16)
        fl = qvf_t[2 * attn_w:2 * attn_w + heads] + bf_ref[...]
        lf_ref[:, r] = jnp.minimum(fl, 0.0) - jnp.log1p(jnp.exp(-jnp.abs(fl)))
    for r, h in zip(halves, hs):
        rest = jnp.dot(h, wrest_ref[...], preferred_element_type=F32)
        k_ref[r, :] = rest[:, :attn_w].astype(BF16)
        ga = rest[:, attn_w:attn_w + conv_w]
        gb = rest[:, attn_w + conv_w:]
        u_ref[r, :] = ga * (1.0 / (1.0 + jnp.exp(-gb)))


def _in_proj(x2, g, wqv_t, wrest, bf, to_cast, *, attn_w, conv_w, tm):
    n, d = x2.shape
    heads = bf.shape[0]
    steps = n // tm
    const = lambda i: (0, 0)
    row = lambda i: (i, 0)
    col = lambda i: (0, i)
    cast_specs = [pl.BlockSpec((a.shape[0] // steps, a.shape[1]), row) for a in to_cast]
    assert all(a.shape[0] % (steps * 2 * SUBLANES) == 0 for a in to_cast)
    outs = pl.pallas_call(
        functools.partial(_in_proj_kernel, attn_w=attn_w, conv_w=conv_w, parts=IN_PROJ_PARTS,
                          n_cast=len(to_cast)),
        grid=(steps,),
        in_specs=[
            pl.BlockSpec((tm, d), row),
            pl.BlockSpec((1, d), const),
            pl.BlockSpec(wqv_t.shape, const),
            pl.BlockSpec(wrest.shape, const),
            pl.BlockSpec(bf.shape, const),
        ] + cast_specs,
        out_specs=[
            pl.BlockSpec((attn_w, tm), col),
            pl.BlockSpec((tm, attn_w), row),
            pl.BlockSpec((attn_w, tm), col),
            pl.BlockSpec((tm, conv_w), row),
            pl.BlockSpec((heads, tm), col),
        ] + cast_specs,
        out_shape=[
            jax.ShapeDtypeStruct((attn_w, n), BF16),
            jax.ShapeDtypeStruct((n, attn_w), BF16),
            jax.ShapeDtypeStruct((attn_w, n), BF16),
            jax.ShapeDtypeStruct((n, conv_w), F32),
            jax.ShapeDtypeStruct((heads, n), F32),
        ] + [jax.ShapeDtypeStruct(a.shape, BF16) for a in to_cast],
        compiler_params=pltpu.CompilerParams(
            dimension_semantics=("arbitrary",), vmem_limit_bytes=VMEM_LIMIT),
        name="in_proj",
    )(x2, g, wqv_t, wrest, bf, *to_cast)
    return outs[:5], outs[5:]


def _cumsum_kernel(lf_ref, tri_ref, scat_ref, ones_ref, kaug_ref, crow_ref, *, blk, heads):
    assert heads == SUBLANES
    seq = lf_ref.shape[1]
    nblk = seq // blk
    tri_u = tri_ref[...]

    def stack3(x):
        terms = [t.astype(F32) for t in _split3(x)]
        return jnp.concatenate(terms + [jnp.zeros_like(x)], axis=0).astype(BF16)

    local = []
    for r in range(nblk):
        sums = jnp.dot(stack3(lf_ref[:, r * blk:(r + 1) * blk]), tri_u,
                       preferred_element_type=F32)
        local.append(sum(sums[t * heads:(t + 1) * heads, :] for t in range(N_SPLIT)))
    offset = jnp.zeros((heads, 1), F32)
    for r in range(nblk):
        c = local[r] + offset
        offset = c[:, blk - 1:blk]
        c = c * LOG2E
        crow_ref[0, :, r * blk:(r + 1) * blk] = c
        aug = ones_ref[...] + lax.dot_general(stack3(-c), scat_ref[...], TN_DIMS,
                                              preferred_element_type=F32)
        kaug_ref[0, r * blk:(r + 1) * blk, :] = aug.astype(BF16)


def _forget_cumsum(lf_t, b, *, blk):
    heads, n = lf_t.shape
    s = n // b
    tri = (jnp.arange(blk)[:, None] <= jnp.arange(blk)[None, :]).astype(BF16)
    lane = jnp.arange(LANES)
    row = jnp.arange(4 * heads)
    src_t, src_h = row[:, None] // heads, row[:, None] % heads
    scat = ((lane[None, :] == N_SPLIT + N_SPLIT * src_h + src_t) & (src_t < N_SPLIT)).astype(BF16)
    ones_row = (lane < N_SPLIT).astype(F32).reshape(1, LANES)
    const2 = lambda i: (0, 0)
    return pl.pallas_call(
        functools.partial(_cumsum_kernel, blk=blk, heads=heads),
        grid=(b,),
        in_specs=[
            pl.BlockSpec((heads, s), lambda i: (0, i)),
            pl.BlockSpec(tri.shape, const2),
            pl.BlockSpec(scat.shape, const2),
            pl.BlockSpec(ones_row.shape, const2),
        ],
        out_specs=[
            pl.BlockSpec((1, s, LANES), lambda i: (i, 0, 0)),
            pl.BlockSpec((1, heads, s), lambda i: (i, 0, 0)),
        ],
        out_shape=[
            jax.ShapeDtypeStruct((b, s, LANES), BF16),
            jax.ShapeDtypeStruct((b, heads, s), F32),
        ],
        compiler_params=pltpu.CompilerParams(dimension_semantics=("arbitrary",)),
        name="forget_cumsum",
    )(lf_t, tri, scat, ones_row)


def _allmax_sublanes(x):
    for shift in (4, 2, 1):
        x = jnp.maximum(x, pltpu.roll(x, shift, axis=0))
    return x


def _attn_kernel(qt_ref, k_ref, kaug_ref, vt_ref, crow_ref, g_ref, o_ref,
                 qrhs_ref, m_ref, l_ref, acc_ref, *, tq, tk, heads):
    n_q = k_ref.shape[1] // tq
    kt = tq // tk
    rowid = lax.broadcasted_iota(jnp.int32, (LANES, tq), 0)
    key_pos = lax.broadcasted_iota(jnp.int32, (tk, tq), 0)
    qry_pos = lax.broadcasted_iota(jnp.int32, (tk, tq), 1)
    causal = [key_pos + d * tk <= qry_pos for d in range(kt)]

    def prepare(i, slot):
        qcols = slice(i * tq, (i + 1) * tq)
        for h in range(heads):
            pair, e = divmod(h, 2)
            q_pair = qt_ref[pair * LANES:(pair + 1) * LANES, qcols]
            own = (rowid >= HEAD_DIM * e) & (rowid < HEAD_DIM * (e + 1))
            qrhs_ref[slot, h, 0:LANES, :] = jnp.where(own, q_pair, jnp.zeros_like(q_pair))
            c_hi, c_mid, c_lo = (t.astype(F32) for t in _split3(crow_ref[0, h:h + 1, qcols]))
            own_lanes = (rowid >= N_SPLIT * (h + 1)) & (rowid < N_SPLIT * (h + 2))
            aug = jnp.where(rowid == 0, c_hi,
                            jnp.where(rowid == 1, c_mid,
                                      jnp.where(rowid == 2, c_lo,
                                                jnp.where(own_lanes, 1.0, 0.0))))
            qrhs_ref[slot, h, LANES:2 * LANES, :] = aug.astype(BF16)
        m_ref[slot] = jnp.full(m_ref.shape[1:], NEG_BIG, F32)
        l_ref[slot] = jnp.zeros(l_ref.shape[1:], F32)
        acc_ref[slot] = jnp.zeros(acc_ref.shape[1:], F32)

    ones_rows = jnp.ones((2 * SUBLANES, tk), BF16)

    def scores(i, j, h):
        keys = slice(j * tk, (j + 1) * tk)
        pair = h // 2
        k_lhs = jnp.concatenate(
            [k_ref[0, keys, pair * LANES:(pair + 1) * LANES], kaug_ref[0, keys, :]], axis=1)
        return jnp.dot(k_lhs, qrhs_ref[i % 2, h], preferred_element_type=F32)

    def absorb(i, j, h, s_t):
        slot = i % 2
        keys = slice(j * tk, (j + 1) * tk)
        if j >= i * kt:
            s_t = jnp.where(causal[j - i * kt], s_t, NEG_BIG)
        s3 = s_t.reshape(tk // SUBLANES, SUBLANES, tq)
        m_old = m_ref[slot, h]
        m_new = jnp.maximum(m_old, _allmax_sublanes(jnp.max(s3, axis=0)))
        alpha = jnp.exp2(m_old - m_new)
        p = jnp.exp2(s3 - m_new[None]).reshape(tk, tq).astype(BF16)
        rows = slice(h * HEAD_DIM, (h + 1) * HEAD_DIM)
        v_aug = jnp.concatenate([vt_ref[rows, keys], ones_rows], axis=0)
        pv = jnp.dot(v_aug, p, preferred_element_type=F32)
        l_ref[slot, h] = alpha * l_ref[slot, h] + pv[HEAD_DIM:HEAD_DIM + SUBLANES, :]
        acc3 = acc_ref[slot, rows, :].reshape(HEAD_DIM // SUBLANES, SUBLANES, tq)
        pv3 = pv[:HEAD_DIM, :].reshape(HEAD_DIM // SUBLANES, SUBLANES, tq)
        acc_ref[slot, rows, :] = (alpha[None] * acc3 + pv3).reshape(HEAD_DIM, tq)
        m_ref[slot, h] = m_new

    def finish(i, slot):
        qcols = slice(i * tq, (i + 1) * tq)
        for h in range(heads):
            rows = slice(h * HEAD_DIM, (h + 1) * HEAD_DIM)
            acc3 = acc_ref[slot, rows, :].reshape(HEAD_DIM // SUBLANES, SUBLANES, tq)
            o = (acc3 / l_ref[slot, h][None]).reshape(HEAD_DIM, tq)
            ms = jnp.mean(o * o, axis=0, keepdims=True)
            o_ref[rows, qcols] = (o * lax.rsqrt(ms + EPS) * g_ref[rows, :]).astype(o_ref.dtype)

    chains = [(i, j, h) for i in range(n_q) for j in range((i + 1) * kt) for h in range(heads)]
    lookahead = 10
    assert lookahead <= 2 * heads
    prepared = set()
    pending = {}

    def issue(p):
        if p < len(chains):
            i = chains[p][0]
            if i not in prepared:
                prepare(i, i % 2)
                prepared.add(i)
            pending[p] = scores(*chains[p])

    for p in range(lookahead):
        issue(p)
    for p, (i, j, h) in enumerate(chains):
        absorb(i, j, h, pending.pop(p))
        if j == (i + 1) * kt - 1 and h == heads - 1:
            finish(i, i % 2)
        issue(p + lookahead)


def _attention(q_t, k3, kaug, v_t, c_row, g_col, *, tq, tk):
    b, s, w = k3.shape
    heads = w // HEAD_DIM
    return pl.pallas_call(
        functools.partial(_attn_kernel, tq=tq, tk=tk, heads=heads),
        grid=(b,),
        in_specs=[
            pl.BlockSpec((w, s), lambda bi: (0, bi)),
            pl.BlockSpec((1, s, w), lambda bi: (bi, 0, 0)),
            pl.BlockSpec((1, s, LANES), lambda bi: (bi, 0, 0)),
            pl.BlockSpec((w, s), lambda bi: (0, bi)),
            pl.BlockSpec((1, heads, s), lambda bi: (bi, 0, 0)),
            pl.BlockSpec((w, 1), lambda bi: (0, 0)),
        ],
        out_specs=pl.BlockSpec((w, s), lambda bi: (0, bi)),
        out_shape=jax.ShapeDtypeStruct((w, b * s), BF16),
        scratch_shapes=[
            pltpu.VMEM((2, heads, 2 * LANES, tq), BF16),
            pltpu.VMEM((2, heads, SUBLANES, tq), F32),
            pltpu.VMEM((2, heads, SUBLANES, tq), F32),
            pltpu.VMEM((2, w, tq), F32),
        ],
        compiler_params=pltpu.CompilerParams(
            dimension_semantics=("arbitrary",), vmem_limit_bytes=VMEM_LIMIT),
        name="fox_attention",
    )(q_t, k3, kaug, v_t, c_row, g_col)


def _depthwise_strips(u_ref, halo_ref, dw_ref, db_ref, pad_ref, dwout_ref, *, first_in_seq, sub_rows):
    tm, width = u_ref.shape
    halo = halo_ref.shape[0]
    prev = halo_ref[...]
    pad_ref[0:halo, :] = jnp.where(first_in_seq, jnp.zeros_like(prev), prev)
    pad_ref[halo:, :] = u_ref[...]
    lead = halo - (CONV_KERNEL - 1)
    window = sub_rows + halo

    def strip(sb, cb):
        lanes = slice(cb * LANES, (cb + 1) * LANES)
        x_win = pad_ref[sb * sub_rows:sb * sub_rows + window, lanes]
        acc = jnp.zeros((sub_rows, LANES), F32) + db_ref[:, lanes]
        for res in range(SUBLANES):
            taps = [t for t in range(CONV_KERNEL) if (lead + t) % SUBLANES == res]
            shifted = x_win if res == 0 else pltpu.roll(x_win, window - res, axis=0)
            for t in taps:
                a0 = (lead + t) - res
                acc = acc + shifted[a0:a0 + sub_rows, :] * dw_ref[t:t + 1, lanes]
        dwout_ref[sb * sub_rows:(sb + 1) * sub_rows, lanes] = acc
        return jnp.sum(acc, axis=0, keepdims=True)

    return [functools.partial(strip, sb, cb)
            for sb in range(tm // sub_rows) for cb in range(width // LANES)]


def _norm_swish(x, lg_ref, lb_ref):
    mu = jnp.mean(x, axis=-1, keepdims=True)
    xc = x - mu
    var = jnp.mean(xc * xc, axis=-1, keepdims=True)
    y = xc * lax.rsqrt(var + EPS) * lg_ref[...] + lb_ref[...]
    return y * (1.0 / (1.0 + jnp.exp(-y)))


def _post_kernel(x_ref, at_ref, u_ref, halo_ref, dw_ref, db_ref, lg_ref, lb_ref, pw_ref, pb_ref,
                 og_ref, gsum_ref, woa_ref, woc_ref, gf_ref, wup_ref, wdn_ref, gl_ref,
                 o_ref, pad_ref, dwout_ref, y_ref,
                 *, tiles_per_seq, n_tiles, sub_rows, part_rows, ff_chunks, apply_final_norm):
    g = pl.program_id(0)

    @pl.when(g == 0)
    def _():
        y_ref[...] = jnp.zeros(y_ref.shape, y_ref.dtype)

    tile = jnp.minimum(g, n_tiles - 1)
    first_in_seq = lax.rem(tile, tiles_per_seq) == 0
    strips = _depthwise_strips(u_ref, halo_ref, dw_ref, db_ref, pad_ref, dwout_ref,
                               first_in_seq=first_in_seq, sub_rows=sub_rows)
    tm = o_ref.shape[0]
    ff = wup_ref.shape[1] // ff_chunks
    bounds = [round(w * len(strips) / (ff_chunks - 1)) for w in range(ff_chunks)]
    norm_rows = tm // 4
    top = 2 * SUBLANES

    def norm_block(r):
        rows = slice(r * norm_rows, (r + 1) * norm_rows)
        y = _norm_swish(dwout_ref[rows, :], lg_ref, lb_ref)
        y_ref[rows, :] = y.astype(y_ref.dtype)
        t = jnp.sum(y, axis=0, keepdims=True)
        return sum(t[:, i * LANES:(i + 1) * LANES] for i in range(y.shape[1] // LANES))

    def tied(lhs, work):
        tokens = [item() for item in work]
        t = sum(tokens[1:], tokens[0])
        corner = jnp.where(g >= 0, lhs[:top, :LANES], jnp.broadcast_to(t, (top, LANES)).astype(BF16))
        return jnp.concatenate(
            [jnp.concatenate([corner, lhs[:top, LANES:]], axis=1), lhs[top:, :]], axis=0)

    halves = [slice(i * part_rows, (i + 1) * part_rows) for i in range(tm // part_rows)]
    z = [jnp.dot(y_ref[r, :], pw_ref[...], preferred_element_type=F32) + pb_ref[...] for r in halves]
    x_attn = [x_ref[r, :] + lax.dot_general(at_ref[:, r], woa_ref[...], TN_DIMS,
                                            preferred_element_type=F32) for r in halves]
    h = []
    for i, r in enumerate(halves):
        ms = jnp.dot((z[i] * z[i]).astype(BF16), gsum_ref[...],
                     preferred_element_type=F32) * (1.0 / HEAD_DIM)
        conv = (z[i] * lax.rsqrt(ms + EPS) * og_ref[...]).astype(BF16)
        x1 = x_attn[i] + jnp.dot(conv, woc_ref[...], preferred_element_type=F32)
        o_ref[r, :] = x1
        h.append(_rms(x1, gf_ref[...]).astype(BF16))
    for c in range(ff_chunks):
        cols = slice(c * ff, (c + 1) * ff)
        lhs = [tied(h[0], strips[bounds[c - 1]:bounds[c]]) if c else h[0]] + h[1:]
        a2 = []
        for h_i in lhs:
            a = jnp.maximum(jnp.dot(h_i, wup_ref[:, cols], preferred_element_type=F32), 0.0)
            a2.append((a * a).astype(BF16))
        if c == ff_chunks - 1:
            blocks = [functools.partial(norm_block, r) for r in range(tm // norm_rows)]
            a2[0] = tied(a2[0], blocks[:1])
            a2[-1] = tied(a2[-1], blocks[1:-1])
            blocks[-1]()
        for r, a2_i in zip(halves, a2):
            o_ref[r, :] += jnp.dot(a2_i, wdn_ref[cols, :], preferred_element_type=F32)
    if apply_final_norm:
        for r in halves:
            o_ref[r, :] = _rms(o_ref[r, :], gl_ref[...])


def _post(x2, attn_t, u, dw, db, lg, lb, pw, pb, og, gsum, woa, woc, gf, wup, wdn, gl,
          *, tm, seq, apply_final_norm):
    n, d = x2.shape
    w = u.shape[1]
    halo = 32
    n_tiles = n // tm
    const = lambda g: (0, 0)
    prev_tile = lambda g: (jnp.maximum(g - 1, 0), 0)
    cur_tile = lambda g: (jnp.minimum(g, n_tiles - 1), 0)
    halo_blk = lambda g: (jnp.maximum(jnp.minimum(g, n_tiles - 1) * (tm // halo) - 1, 0), 0)
    vec = pl.BlockSpec((1, w), const)
    resident = lambda a: pl.BlockSpec(a.shape, const, pipeline_mode=pl.Buffered(1))
    return pl.pallas_call(
        functools.partial(_post_kernel, tiles_per_seq=seq // tm, n_tiles=n_tiles,
                          sub_rows=CONV_STRIP_ROWS, part_rows=POST_PART_ROWS, ff_chunks=POST_FF_CHUNKS,
                          apply_final_norm=apply_final_norm),
        grid=(n_tiles + 1,),
        in_specs=[
            pl.BlockSpec((tm, d), prev_tile),
            pl.BlockSpec((attn_t.shape[0], tm), lambda g: (0, jnp.maximum(g - 1, 0))),
            pl.BlockSpec((tm, w), cur_tile),
            pl.BlockSpec((halo, w), halo_blk),
            pl.BlockSpec(dw.shape, const),
            vec, vec, vec,
            pl.BlockSpec(pw.shape, const),
            vec, vec,
            pl.BlockSpec(gsum.shape, const),
            resident(woa),
            resident(woc),
            pl.BlockSpec((1, d), const),
            resident(wup),
            resident(wdn),
            pl.BlockSpec((1, d), const),
        ],
        out_specs=pl.BlockSpec((tm, d), prev_tile),
        out_shape=jax.ShapeDtypeStruct((n, d), F32),
        scratch_shapes=[
            pltpu.VMEM((tm + halo, w), F32),
            pltpu.VMEM((tm, w), F32),
            pltpu.VMEM((tm, w), BF16),
        ],
        compiler_params=pltpu.CompilerParams(
            dimension_semantics=("arbitrary",), vmem_limit_bytes=VMEM_LIMIT),
        name="conv_out_proj_ffn",
    )(x2, attn_t, u, u, dw, db, lg, lb, pw, pb, og, gsum, woa, woc, gf, wup, wdn, gl)


def _group_sum_matrix(width):
    idx = jnp.arange(width) // HEAD_DIM
    return (idx[:, None] == idx[None, :]).astype(BF16)


def kernel(x, norm_mix_g, w_in, b_forget, conv_dw_w, conv_dw_b, conv_ln_g, conv_ln_b,
           w_conv_pw, b_conv_pw, attn_out_g, conv_out_g, w_out, norm_ffn_g,
           w_ffn_up, w_ffn_down, norm_final_g):
    b, s, d = x.shape
    depth = w_in.shape[0]
    heads = b_forget.shape[1]
    attn_w = heads * HEAD_DIM
    conv_w = conv_dw_w.shape[2]
    n = b * s
    x2 = x.reshape(n, d)
    gsum_conv = _group_sum_matrix(conv_w)
    for l in range(depth):
        w = w_in[l]
        wq, wk, wv = w[:, :attn_w], w[:, attn_w:2 * attn_w], w[:, 2 * attn_w:3 * attn_w]
        wf = jnp.pad(w[:, 3 * attn_w + 2 * conv_w:], ((0, 0), (0, 2 * SUBLANES - heads)))
        wqvf_t = jnp.concatenate([wq, wv, wf], axis=1).astype(BF16).T
        wrest = jnp.concatenate([wk, w[:, 3 * attn_w:3 * attn_w + 2 * conv_w]], axis=1).astype(BF16)
        (q_t, k, v_t, u, lf_t), (wo, wup, wdn, wpw) = _in_proj(
            x2, norm_mix_g[l].reshape(1, d), wqvf_t, wrest, b_forget[l].reshape(heads, 1),
            [w_out[l], w_ffn_up[l], w_ffn_down[l], w_conv_pw[l]],
            attn_w=attn_w, conv_w=conv_w, tm=IN_PROJ_ROWS)

        kaug, c_row = _forget_cumsum(lf_t, b, blk=CUMSUM_BLOCK)
        attn_t = _attention(q_t, k.reshape(b, s, attn_w), kaug, v_t, c_row,
                            attn_out_g[l].reshape(attn_w, 1), tq=ATTN_TILE, tk=ATTN_TILE)

        vec = lambda a: a.reshape(1, conv_w)
        x2 = _post(x2, attn_t, u, conv_dw_w[l], vec(conv_dw_b[l]), vec(conv_ln_g[l]),
                   vec(conv_ln_b[l]), wpw, vec(b_conv_pw[l]),
                   vec(conv_out_g[l]), gsum_conv, wo[:attn_w], wo[attn_w:],
                   norm_ffn_g[l].reshape(1, d), wup, wdn, norm_final_g.reshape(1, d),
                   tm=POST_ROWS, seq=s, apply_final_norm=(l == depth - 1))
    return x2.reshape(b, s, d)
```
